```python
import jax, jax.numpy as jnp
from jax import lax
import numpy as np

D_MODEL = 1024
BATCH = 8
SEQ = 2048
DEPTH = 1

HGRN_HEADS = 8
HGRN_EXPAND = 128
HGRN_FWD = HGRN_HEADS * HGRN_EXPAND
HGRN_IN = D_MODEL
HGRN_VDIM = HGRN_IN // HGRN_HEADS
HGRN_SCALE = HGRN_EXPAND ** -0.5
CHUNK = 32
RWKV_HEAD = 64
RWKV_DIM = D_MODEL
RWKV_HEADS = RWKV_DIM // RWKV_HEAD
W_LORA = 64
A_LORA = 64
G_LORA = 128
GN_EPS = 1e-5 * RWKV_HEAD
D_FF = 2816
CONV_W = 3
EPS = 1e-6

HGRN_COLS = 2 * HGRN_FWD + 2 * HGRN_IN
RWKV_COLS = 3 * RWKV_DIM + W_LORA + A_LORA + G_LORA
GATE_COLS = 2 * D_MODEL
IN_COLS = HGRN_COLS + RWKV_COLS + GATE_COLS

kernel_name = "hgrn2_rwkv7_gated_hybrid_convffn"

F32 = jnp.float32


def _rmsnorm(x, g):
    xf = x.astype(F32)
    y = xf * lax.rsqrt(jnp.mean(xf * xf, axis=-1, keepdims=True) + EPS)
    return (y * g.astype(F32)).astype(x.dtype)


def _split(z, sizes):
    outs, off = [], 0
    for s in sizes:
        outs.append(z[..., off:off + s])
        off += s
    return outs


def _shift1(z):
    return jnp.pad(z[:, :-1], ((0, 0), (1, 0), (0, 0)))


def _causal_dwconv(h, w, b):
    S = h.shape[1]
    hp = jnp.pad(h, ((0, 0), (CONV_W - 1, 0), (0, 0)))
    out = b
    for j in range(CONV_W):
        out = out + w[j] * hp[:, j:j + S]
    return out


def _hgrn2_chunkwise(q, f_log, k, v):
    B, S, H, K = q.shape
    V = v.shape[-1]
    N = S // CHUNK

    def chunk(t):
        return t.reshape(B, N, CHUNK, H, t.shape[-1]).transpose(0, 3, 1, 2, 4)

    qc, gc, kc, vc = chunk(q), chunk(f_log), chunk(k), chunk(v)
    b = jnp.cumsum(gc, axis=3)
    b_ref = b[:, :, :, CHUNK // 2 - 1:CHUNK // 2, :]
    q_in = qc * jnp.exp(b - b_ref)
    k_in = kc * jnp.exp(b_ref - b)
    scores = jnp.einsum('bhnck,bhndk->bhncd', q_in, k_in)
    mask = jnp.tril(jnp.ones((CHUNK, CHUNK), dtype=bool))
    scores = jnp.where(mask, scores, 0.0)
    o_intra = jnp.einsum('bhncd,bhndv->bhncv', scores, vc)

    b_last = b[:, :, :, -1, :]
    u = jnp.einsum('bhnck,bhncv->bhnkv', kc * jnp.exp(b_last[:, :, :, None, :] - b), vc)
    decay = jnp.exp(b_last)

    def step(state, inp):
        d, u_n = inp
        return d[..., None] * state + u_n, state

    s0 = jnp.zeros((B, H, K, V), F32)
    _, s_prev = lax.scan(step, s0, (decay.transpose(2, 0, 1, 3), u.transpose(2, 0, 1, 3, 4)))
    s_prev = s_prev.transpose(1, 2, 0, 3, 4)
    o_inter = jnp.einsum('bhnck,bhnkv->bhncv', qc * jnp.exp(b), s_prev)
    o = o_intra + o_inter
    return o.transpose(0, 2, 3, 1, 4).reshape(B, S, H, V)


def _rwkv7_scan(r, w, k, v, a_vec, b_vec):
    B, S, H, N = r.shape

    def step(state, inp):
        r_t, w_t, k_t, v_t, a_t, b_t = inp
        sa = jnp.einsum('bhvk,bhk->bhv', state, a_t)
        state = (state * w_t[:, :, None, :] + sa[..., None] * b_t[:, :, None, :]
                 + v_t[..., None] * k_t[:, :, None, :])
        y = jnp.einsum('bhvk,bhk->bhv', state, r_t)
        return state, y

    xs = tuple(t.astype(F32).transpose(1, 0, 2, 3) for t in (r, w, k, v, a_vec, b_vec))
    _, y = lax.scan(step, jnp.zeros((B, H, N, N), F32), xs)
    return y.transpose(1, 0, 2, 3)


def _token_mixer(xn, lb, w_in, hgrn_gnorm, w_branch_a, rwkv_mu, rwkv_w0, rwkv_w2, rwkv_a0,
                 rwkv_a2, rwkv_g2, rwkv_k_k, rwkv_k_a, rwkv_r_k, rwkv_ln_w, rwkv_ln_b,
                 w_branch_b, w_out):
    B, S, _ = xn.shape
    z = xn @ w_in
    z_h, z_r, z_g = _split(z, [HGRN_COLS, RWKV_COLS, GATE_COLS])

    hq, hf, hi, hg = _split(z_h, [HGRN_FWD, HGRN_FWD, HGRN_IN, HGRN_IN])
    q = jax.nn.silu(hq.astype(F32)).reshape(B, S, HGRN_HEADS, HGRN_EXPAND) * HGRN_SCALE
    f = lb + (1.0 - lb) * jax.nn.sigmoid(hf.astype(F32))
    f = f.reshape(B, S, HGRN_HEADS, HGRN_EXPAND)
    k_h = 1.0 - f
    vi = hi.astype(F32).reshape(B, S, HGRN_HEADS, HGRN_VDIM)
    o_a = _hgrn2_chunkwise(q, jnp.log(f), k_h, vi)
    o_a = o_a * lax.rsqrt(jnp.mean(o_a * o_a, axis=-1, keepdims=True) + EPS)
    o_a = o_a * hgrn_gnorm.astype(F32).reshape(HGRN_HEADS, HGRN_VDIM)
    o_a = o_a.reshape(B, S, HGRN_IN) * jax.nn.silu(hg.astype(F32))
    y_a = o_a.astype(xn.dtype) @ w_branch_a

    z_r = z_r + rwkv_mu * (_shift1(z_r) - z_r)
    rr, kr, vr, wz, az, gz = _split(z_r, [RWKV_DIM, RWKV_DIM, RWKV_DIM, W_LORA, A_LORA, G_LORA])
    w_log = -jax.nn.softplus(-(rwkv_w0 + jnp.tanh(wz) @ rwkv_w2).astype(F32)) - 0.5
    decay = jnp.exp(-jnp.exp(w_log))
    a = jax.nn.sigmoid((rwkv_a0 + az @ rwkv_a2).astype(F32))
    g = jax.nn.sigmoid(gz) @ rwkv_g2
    kr = kr.astype(F32)
    kk = (kr * rwkv_k_k).reshape(B, S, RWKV_HEADS, RWKV_HEAD)
    kk = kk / jnp.maximum(jnp.linalg.norm(kk, axis=-1, keepdims=True), 1e-12)
    kr = kr * (1.0 + (a - 1.0) * rwkv_k_a)

    def heads(t):
        return t.astype(F32).reshape(B, S, RWKV_HEADS, RWKV_HEAD)

    r_h, k_r, v_r, w_h, a_h = heads(rr), heads(kr), heads(vr), heads(decay), heads(a)
    y = _rwkv7_scan(r_h, w_h, k_r, v_r, -kk, kk * a_h)
    mu_y = jnp.mean(y, axis=-1, keepdims=True)
    var_y = jnp.mean(jnp.square(y - mu_y), axis=-1, keepdims=True)
    y = ((y - mu_y) * lax.rsqrt(var_y + GN_EPS)).reshape(B, S, RWKV_DIM)
    y = y * rwkv_ln_w.astype(F32) + rwkv_ln_b.astype(F32)
    bonus = jnp.sum(r_h * k_r * rwkv_r_k.astype(F32), axis=-1, keepdims=True) * v_r
    o_b = (y + bonus.reshape(B, S, RWKV_DIM)) * g.astype(F32)
    y_b = o_b.astype(xn.dtype) @ w_branch_b

    ga, gb = _split(z_g, [D_MODEL, D_MODEL])
    merged = jax.nn.sigmoid(ga) * y_a + jax.nn.sigmoid(gb) * y_b
    return merged @ w_out


def _conv_ffn(xn, w_up, conv_w, conv_b, w_down):
    hu = xn @ w_up
    hc = _causal_dwconv(hu, conv_w, conv_b)
    gate, val = _split(hc, [D_FF, D_FF])
    return (jax.nn.silu(gate) * val) @ w_down


def setup_inputs(seed: int = 0) -> dict:
    key = jax.random.key(seed)
    ks = jax.random.split(key, 32)
    L = DEPTH

    def nrm(k, shape, scale):
        return jax.random.normal(k, shape, F32) * scale

    def gain(k, shape):
        return 1.0 + 0.02 * jax.random.normal(k, shape, F32)

    return {
        "x": jax.random.normal(ks[0], (BATCH, SEQ, D_MODEL), F32),
        "attn_pre_norm": gain(ks[1], (L, D_MODEL)),
        "w_in": nrm(ks[2], (L, D_MODEL, IN_COLS), D_MODEL ** -0.5),
        "hgrn_lb": nrm(ks[3], (DEPTH + 1, HGRN_FWD), 0.1),
        "hgrn_gnorm": gain(ks[4], (L, HGRN_IN)),
        "w_branch_a": nrm(ks[5], (L, HGRN_IN, D_MODEL), HGRN_IN ** -0.5),
        "rwkv_mu": jax.random.uniform(ks[6], (L, RWKV_COLS), F32),
        "rwkv_w0": jax.random.uniform(ks[7], (L, RWKV_DIM), F32, minval=-6.0, maxval=0.0),
        "rwkv_w2": nrm(ks[8], (L, W_LORA, RWKV_DIM), 0.1 * W_LORA ** -0.5),
        "rwkv_a0": nrm(ks[9], (L, RWKV_DIM), 0.1),
        "rwkv_a2": nrm(ks[10], (L, A_LORA, RWKV_DIM), 0.5 * A_LORA ** -0.5),
        "rwkv_g2": nrm(ks[11], (L, G_LORA, RWKV_DIM), G_LORA ** -0.5),
        "rwkv_k_k": 0.85 + 0.02 * jax.random.normal(ks[12], (L, RWKV_DIM), F32),
        "rwkv_k_a": gain(ks[13], (L, RWKV_DIM)),
        "rwkv_r_k": nrm(ks[14], (L, RWKV_HEADS, RWKV_HEAD), 0.1),
        "rwkv_ln_w": gain(ks[15], (L, RWKV_DIM)),
        "rwkv_ln_b": nrm(ks[16], (L, RWKV_DIM), 0.01),
        "w_branch_b": nrm(ks[17], (L, RWKV_DIM, D_MODEL), RWKV_DIM ** -0.5),
        "w_out": nrm(ks[18], (L, D_MODEL, D_MODEL), D_MODEL ** -0.5),
        "attn_post_norm": gain(ks[19], (L, D_MODEL)),
        "ffn_pre_norm": gain(ks[20], (L, D_MODEL)),
        "w_up": nrm(ks[21], (L, D_MODEL, 2 * D_FF), D_MODEL ** -0.5),
        "conv_w": nrm(ks[22], (L, CONV_W, 2 * D_FF), CONV_W ** -0.5),
        "conv_b": nrm(ks[23], (L, 2 * D_FF), 0.01),
        "w_down": nrm(ks[24], (L, D_FF, D_MODEL), D_FF ** -0.5),
        "ffn_post_norm": gain(ks[25], (L, D_MODEL)),
    }


def reference(x, attn_pre_norm, w_in, hgrn_lb, hgrn_gnorm, w_branch_a, rwkv_mu, rwkv_w0,
              rwkv_w2, rwkv_a0, rwkv_a2, rwkv_g2, rwkv_k_k, rwkv_k_a, rwkv_r_k, rwkv_ln_w,
              rwkv_ln_b, w_branch_b, w_out, attn_post_norm, ffn_pre_norm, w_up, conv_w,
              conv_b, w_down, ffn_post_norm):
    lb_table = jnp.cumsum(jax.nn.softmax(hgrn_lb.astype(F32), axis=0), axis=0)
    h = x
    for l in range(DEPTH):
        xn = _rmsnorm(h, attn_pre_norm[l])
        mix = _token_mixer(xn, lb_table[l], w_in[l], hgrn_gnorm[l], w_branch_a[l], rwkv_mu[l],
                           rwkv_w0[l], rwkv_w2[l], rwkv_a0[l], rwkv_a2[l], rwkv_g2[l],
                           rwkv_k_k[l], rwkv_k_a[l], rwkv_r_k[l], rwkv_ln_w[l], rwkv_ln_b[l],
                           w_branch_b[l], w_out[l])
        h = h + _rmsnorm(mix, attn_post_norm[l])
        xn = _rmsnorm(h, ffn_pre_norm[l])
        ff = _conv_ffn(xn, w_up[l], conv_w[l], conv_b[l], w_down[l])
        h = h + _rmsnorm(ff, ffn_post_norm[l])
    return h
```

```python
import functools

import jax
import jax.numpy as jnp
import numpy as np
from jax import lax
from jax.experimental import pallas as pl
from jax.experimental.pallas import tpu as pltpu

F32 = jnp.float32
BF16 = jnp.bfloat16

D_MODEL = 1024
HGRN_HEADS = 8
HGRN_EXPAND = 128
HGRN_FWD = HGRN_HEADS * HGRN_EXPAND
HGRN_IN = D_MODEL
HGRN_SCALE = HGRN_EXPAND ** -0.5
CHUNK = 32
RWKV_HEAD = 64
RWKV_DIM = D_MODEL
W_LORA = 64
A_LORA = 64
G_LORA = 128
GN_EPS = 1e-5 * RWKV_HEAD
D_FF = 2816
CONV_W = 3
EPS = 1e-6
HGRN_COLS = 2 * HGRN_FWD + 2 * HGRN_IN
RWKV_COLS = 3 * RWKV_DIM + W_LORA + A_LORA + G_LORA
GATE_COLS = 2 * D_MODEL

SUBLANES = 8
LANES = 128
MXU_WIDTH = 256
VMEM_LIMIT_BYTES = 56 * 1024 * 1024

HGRN_CHUNK_GROUP = 8
FFN_COL_TILE = 256


def _sigmoid(x):
    return 1.0 / (1.0 + jnp.exp(-x))


def _silu(x):
    return x * _sigmoid(x)


def _bdot(a, b):
    return jnp.dot(a.astype(BF16), b.astype(BF16), preferred_element_type=F32)


def _segment_sum_bcast(x, e_ref):
    n = x.shape[-1]
    e = e_ref[...]
    outs = []
    for j in range(n // MXU_WIDTH):
        outs.append(jnp.dot(x[:, j * MXU_WIDTH:(j + 1) * MXU_WIDTH].astype(BF16), e,
                            preferred_element_type=F32))
    return jnp.concatenate(outs, axis=-1)


def _cparams(semantics):
    return pltpu.CompilerParams(dimension_semantics=semantics,
                                vmem_limit_bytes=VMEM_LIMIT_BYTES)


def _inproj_kernel(x_ref, g_ref, w_ref, o_ref, *, act):
    x = x_ref[...]
    ms = jnp.mean(x * x, axis=-1, keepdims=True)
    xn = x * lax.rsqrt(ms + EPS) * g_ref[...]
    z = jnp.dot(xn.astype(BF16), w_ref[...], preferred_element_type=F32)
    if act == "sigmoid":
        z = _sigmoid(z)
    o_ref[...] = z.astype(o_ref.dtype)


def _inproj(x2, gain, w, *, act, out_dtype, tm, name):
    m, d = x2.shape
    n = w.shape[1]
    return pl.pallas_call(
        functools.partial(_inproj_kernel, act=act),
        out_shape=jax.ShapeDtypeStruct((m, n), out_dtype),
        grid=(m // tm,),
        in_specs=[
            pl.BlockSpec((tm, d), lambda i: (i, 0)),
            pl.BlockSpec((1, d), lambda i: (0, 0)),
            pl.BlockSpec((d, n), lambda i: (0, 0)),
        ],
        out_specs=pl.BlockSpec((tm, n), lambda i: (i, 0)),
        compiler_params=_cparams(("arbitrary",)),
        name=name,
    )(x2, gain, w)


def _hgrn_kernel(hq_ref, hf_ref, hi_ref, hg_ref, lb_ref, gn_ref, o_ref, state_ref, *, layer):
    s_len = hq_ref.shape[1]
    rows = HGRN_CHUNK_GROUP * CHUNK
    n_groups = s_len // rows

    lbp = lb_ref[...]
    lbm = jnp.max(lbp, axis=0, keepdims=True)
    lbe = jnp.exp(lbp - lbm)
    lb = (jnp.sum(lbe[0:layer + 1, :], axis=0, keepdims=True)
          / jnp.sum(lbe, axis=0, keepdims=True))
    gn = gn_ref[...]

    state_ref[...] = jnp.zeros_like(state_ref)

    row_in_chunk = lax.broadcasted_iota(jnp.int32, (rows, HGRN_EXPAND), 0) % CHUNK
    ci = lax.broadcasted_iota(jnp.int32, (CHUNK, CHUNK), 0)
    di = lax.broadcasted_iota(jnp.int32, (CHUNK, CHUNK), 1)
    tril = ci >= di

    def group(gi, carry):
        r0 = pl.multiple_of(gi * rows, rows)
        hq = hq_ref[0, pl.ds(r0, rows), :]
        hf = hf_ref[0, pl.ds(r0, rows), :]
        v = hi_ref[0, pl.ds(r0, rows), :]
        hg = hg_ref[0, pl.ds(r0, rows), :]
        q = _silu(hq) * HGRN_SCALE
        f = lb + (1.0 - lb) * _sigmoid(hf)
        k = 1.0 - f
        b = jnp.log(f)
        sh = 1
        while sh < CHUNK:
            b = b + jnp.where(row_in_chunk >= sh, pltpu.roll(b, sh, axis=0), 0.0)
            sh *= 2
        outs = []
        st = state_ref[...]
        for j in range(HGRN_CHUNK_GROUP):
            sl = slice(j * CHUNK, (j + 1) * CHUNK)
            bj, qj, kj, vj = b[sl], q[sl], k[sl], v[sl]
            b_mid = bj[CHUNK // 2 - 1:CHUNK // 2, :]
            b_last = bj[CHUNK - 1:CHUNK, :]
            q_in = qj * jnp.exp(bj - b_mid)
            k_in = kj * jnp.exp(b_mid - bj)
            scores = lax.dot_general(q_in.astype(BF16), k_in.astype(BF16),
                                     (((1,), (1,)), ((), ())), preferred_element_type=F32)
            scores = jnp.where(tril, scores, 0.0)
            o_intra = _bdot(scores, vj)
            q_dec = qj * jnp.exp(bj)
            o_inter = lax.dot_general(q_dec.astype(BF16), st.astype(BF16),
                                      (((1,), (1,)), ((), ())), preferred_element_type=F32)
            k_dec = kj * jnp.exp(b_last - bj)
            u_t = _bdot(vj.T, k_dec)
            st = st * jnp.exp(b_last) + u_t
            outs.append(o_intra + o_inter)
        state_ref[...] = st
        o = jnp.concatenate(outs, axis=0)
        o = o * lax.rsqrt(jnp.mean(o * o, axis=-1, keepdims=True) + EPS)
        o = o * gn * _silu(hg)
        o_ref[0, pl.ds(r0, rows), :] = o.astype(o_ref.dtype)
        return carry

    lax.fori_loop(0, n_groups, group, 0)


def _hgrn(z_h, hgrn_lb, gnorm, layer, out_dtype):
    bsz, s_len, _ = z_h.shape
    nb = HGRN_FWD // HGRN_EXPAND

    def col(off):
        return pl.BlockSpec((1, s_len, HGRN_EXPAND), lambda b, h: (b, 0, off + h))

    return pl.pallas_call(
        functools.partial(_hgrn_kernel, layer=layer),
        out_shape=jax.ShapeDtypeStruct((bsz, s_len, HGRN_IN), out_dtype),
        grid=(bsz, HGRN_HEADS),
        in_specs=[
            col(0), col(nb), col(2 * nb), col(3 * nb),
            pl.BlockSpec((hgrn_lb.shape[0], HGRN_EXPAND), lambda b, h: (0, h)),
            pl.BlockSpec((1, HGRN_EXPAND), lambda b, h: (0, h)),
        ],
        out_specs=pl.BlockSpec((1, s_len, HGRN_EXPAND), lambda b, h: (b, 0, h)),
        scratch_shapes=[pltpu.VMEM((HGRN_EXPAND, HGRN_EXPAND), F32)],
        compiler_params=_cparams(("arbitrary", "arbitrary")),
        name="hgrn2_chunkwise",
    )(z_h, z_h, z_h, z_h, hgrn_lb, gnorm)


def _rwkv_prep_kernel(z_ref, mu_ref, w0_ref, a0_ref, kk_ref, ka_ref, rk_ref, wa2_ref, g2_ref,
                      e_ref, r_out, w_out, k_out, v_out, kk_out, kb_out, g_out, bonus_out,
                      carry_ref):
    tm = z_ref.shape[1]

    @pl.when(pl.program_id(1) == 0)
    def _():
        carry_ref[...] = jnp.zeros_like(carry_ref)

    z = z_ref[0]
    prev = carry_ref[0:1, :]
    rows = lax.broadcasted_iota(jnp.int32, z.shape, 0)
    zs = jnp.where(rows == 0, prev, pltpu.roll(z, 1, axis=0))
    carry_ref[0:1, :] = z[tm - 1:tm, :]
    zm = z + mu_ref[...] * (zs - z)

    d = RWKV_DIM
    rr = zm[:, 0:d]
    kr = zm[:, d:2 * d]
    vr = zm[:, 2 * d:3 * d]
    wa = zm[:, 3 * d:3 * d + W_LORA + A_LORA]
    gz = zm[:, 3 * d + W_LORA + A_LORA:]
    lane = lax.broadcasted_iota(jnp.int32, wa.shape, 1)
    wa = jnp.where(lane < W_LORA, jnp.tanh(wa), wa)
    wa_out = _bdot(wa, wa2_ref[...])
    w_pre = w0_ref[...] + wa_out[:, 0:d]
    a_pre = a0_ref[...] + wa_out[:, d:2 * d]
    decay = jnp.exp(-float(np.exp(-0.5)) * _sigmoid(w_pre))
    a = _sigmoid(a_pre)
    g = _bdot(_sigmoid(gz), g2_ref[...])

    kk = kr * kk_ref[...]
    ss = _segment_sum_bcast(kk * kk, e_ref)
    kk = kk / jnp.maximum(jnp.sqrt(ss), 1e-12)
    k2 = kr * (1.0 + (a - 1.0) * ka_ref[...])
    bonus = _segment_sum_bcast(rr * k2 * rk_ref[...], e_ref) * vr

    r_out[0] = rr
    w_out[0] = decay
    k_out[0] = k2
    v_out[0] = vr
    kk_out[0] = kk
    kb_out[0] = kk * a
    g_out[0] = g
    bonus_out[0] = bonus


def _rwkv_prep(z_r, mu, w0, a0, k_k, k_a, r_k, wa2, g2, e_mat, tm):
    bsz, s_len, _ = z_r.shape
    d = RWKV_DIM
    row = lambda n: pl.BlockSpec((1, n), lambda b, i: (0, 0))
    out_spec = pl.BlockSpec((1, tm, d), lambda b, i: (b, i, 0))
    out_shape = jax.ShapeDtypeStruct((bsz, s_len, d), F32)
    return pl.pallas_call(
        _rwkv_prep_kernel,
        out_shape=[out_shape] * 8,
        grid=(bsz, s_len // tm),
        in_specs=[
            pl.BlockSpec((1, tm, RWKV_COLS), lambda b, i: (b, i, 0)),
            row(RWKV_COLS), row(d), row(d), row(d), row(d), row(d),
            pl.BlockSpec((W_LORA + A_LORA, 2 * d), lambda b, i: (0, 0)),
            pl.BlockSpec((G_LORA, d), lambda b, i: (0, 0)),
            pl.BlockSpec((MXU_WIDTH, MXU_WIDTH), lambda b, i: (0, 0)),
        ],
        out_specs=[out_spec] * 8,
        scratch_shapes=[pltpu.VMEM((SUBLANES, RWKV_COLS), F32)],
        compiler_params=_cparams(("arbitrary", "arbitrary")),
        name="rwkv7_prep",
    )(z_r, mu, w0, a0, k_k, k_a, r_k, wa2, g2, e_mat)


def _rwkv_scan_kernel(r_ref, w_ref, k_ref, v_ref, kk_ref, kb_ref, e_ref, eye_ref, y_ref, s_ref):
    bsz, tb, d = r_ref.shape
    n = RWKV_HEAD

    @pl.when(pl.program_id(0) == 0)
    def _():
        s_ref[...] = jnp.zeros_like(s_ref)

    def seg(x3):
        return _segment_sum_bcast(x3.reshape(bsz * n, d), e_ref).reshape(bsz, n, d)

    def step(t, carry):
        row = lambda ref: ref[:, pl.ds(t, 1), :]
        eye = eye_ref[...][None]
        s = s_ref[...]
        sa = seg(s * (-row(kk_ref)))
        vb = seg(eye * row(v_ref))
        s = s * row(w_ref) + sa * row(kb_ref) + vb * row(k_ref)
        s_ref[...] = s
        yb = seg(s * row(r_ref))
        y_ref[:, pl.ds(t, 1), :] = jnp.sum(yb * eye, axis=1, keepdims=True)
        return carry

    lax.fori_loop(0, tb, step, 0)


def _rwkv_scan(r, w, k, v, kk, kb, e_mat, eye, tb):
    bsz, s_len, d = r.shape
    blk = pl.BlockSpec((bsz, tb, d), lambda i: (0, i, 0))
    return pl.pallas_call(
        _rwkv_scan_kernel,
        out_shape=jax.ShapeDtypeStruct((bsz, s_len, d), F32),
        grid=(s_len // tb,),
        in_specs=[blk] * 6 + [
            pl.BlockSpec((MXU_WIDTH, MXU_WIDTH), lambda i: (0, 0)),
            pl.BlockSpec((RWKV_HEAD, d), lambda i: (0, 0)),
        ],
        out_specs=blk,
        scratch_shapes=[pltpu.VMEM((bsz, RWKV_HEAD, d), F32)],
        compiler_params=_cparams(("arbitrary",)),
        name="rwkv7_scan",
    )(r, w, k, v, kk, kb, e_mat, eye)


def _merge_kernel(x_ref, oa_ref, y_ref, bonus_ref, g_ref, ga_ref, gb_ref, lnw_ref, lnb_ref,
                  pn_ref, wa_ref, wb_ref, wo_ref, e_ref, o_ref):
    y = y_ref[...]
    inv_n = 1.0 / RWKV_HEAD
    mu = _segment_sum_bcast(y, e_ref) * inv_n
    yc = y - mu
    var = _segment_sum_bcast(yc * yc, e_ref) * inv_n
    yn = yc * lax.rsqrt(var + GN_EPS) * lnw_ref[...] + lnb_ref[...]
    o_b = (yn + bonus_ref[...]) * g_ref[...]
    y_b = jnp.dot(o_b.astype(BF16), wb_ref[...], preferred_element_type=F32)
    y_a = jnp.dot(oa_ref[...].astype(BF16), wa_ref[...], preferred_element_type=F32)
    merged = ga_ref[...].astype(F32) * y_a + gb_ref[...].astype(F32) * y_b
    mix = jnp.dot(merged.astype(BF16), wo_ref[...], preferred_element_type=F32)
    ms = jnp.mean(mix * mix, axis=-1, keepdims=True)
    o_ref[...] = x_ref[...] + mix * lax.rsqrt(ms + EPS) * pn_ref[...]


def _merge(x2, o_a, y, bonus, g, gates, ln_w, ln_b, post_norm, w_a, w_b, w_o, e_mat, tm):
    m, d = x2.shape
    blk = pl.BlockSpec((tm, d), lambda i: (i, 0))
    row = pl.BlockSpec((1, d), lambda i: (0, 0))
    wsp = pl.BlockSpec((d, d), lambda i: (0, 0))
    return pl.pallas_call(
        _merge_kernel,
        out_shape=jax.ShapeDtypeStruct((m, d), F32),
        grid=(m // tm,),
        in_specs=[blk, blk, blk, blk, blk,
                  pl.BlockSpec((tm, d), lambda i: (i, 0)),
                  pl.BlockSpec((tm, d), lambda i: (i, 1)),
                  row, row, row, wsp, wsp, wsp,
                  pl.BlockSpec((MXU_WIDTH, MXU_WIDTH), lambda i: (0, 0))],
        out_specs=blk,
        compiler_params=_cparams(("arbitrary",)),
        name="merge_outproj",
    )(x2, o_a, y, bonus, g, gates, gates, ln_w, ln_b, post_norm, w_a, w_b, w_o, e_mat)


def _ffn_kernel(h_ref, pre_ref, post_ref, wup_ref, cw_ref, cb_ref, wdn_ref, o_ref,
                xn_ref, buf_ref, carry_ref, act_ref):
    tm = h_ref.shape[1]
    ct = FFN_COL_TILE
    n_tiles = D_FF // ct
    halo = SUBLANES

    @pl.when(pl.program_id(1) == 0)
    def _():
        carry_ref[...] = jnp.zeros_like(carry_ref)

    h = h_ref[0]
    ms = jnp.mean(h * h, axis=-1, keepdims=True)
    xn_ref[...] = (h * lax.rsqrt(ms + EPS) * pre_ref[...]).astype(BF16)

    def conv(c0, slot):
        cols = slice(c0, c0 + ct)
        hu = jnp.dot(xn_ref[...], wup_ref[:, cols], preferred_element_type=F32)
        buf_ref[slot, 0:halo, :] = carry_ref[:, cols]
        buf_ref[slot, halo:halo + tm, :] = hu
        carry_ref[:, cols] = hu[tm - halo:tm, :]
        cw = cw_ref[:, cols]
        out = cb_ref[:, cols] + cw[CONV_W - 1:CONV_W, :] * hu
        for j in range(CONV_W - 1):
            back = CONV_W - 1 - j
            out = out + cw[j:j + 1, :] * buf_ref[slot, halo - back:halo - back + tm, :]
        return out

    for i in range(n_tiles):
        gate = conv(i * ct, 0)
        val = conv(D_FF + i * ct, 1)
        act_ref[:, i * ct:(i + 1) * ct] = (_silu(gate) * val).astype(BF16)

    ff = jnp.dot(act_ref[...], wdn_ref[...], preferred_element_type=F32)
    ms2 = jnp.mean(ff * ff, axis=-1, keepdims=True)
    o_ref[0] = h + ff * lax.rsqrt(ms2 + EPS) * post_ref[...]


def _ffn(h1, pre, post, w_up, conv_w, conv_b, w_down, tm):
    bsz, s_len, d = h1.shape
    blk = pl.BlockSpec((1, tm, d), lambda b, i: (b, i, 0))
    row = pl.BlockSpec((1, d), lambda b, i: (0, 0))
    return pl.pallas_call(
        _ffn_kernel,
        out_shape=jax.ShapeDtypeStruct((bsz, s_len, d), F32),
        grid=(bsz, s_len // tm),
        in_specs=[blk, row, row,
                  pl.BlockSpec((d, 2 * D_FF), lambda b, i: (0, 0)),
                  pl.BlockSpec((CONV_W, 2 * D_FF), lambda b, i: (0, 0)),
                  pl.BlockSpec((1, 2 * D_FF), lambda b, i: (0, 0)),
                  pl.BlockSpec((D_FF, d), lambda b, i: (0, 0))],
        out_specs=blk,
        scratch_shapes=[
            pltpu.VMEM((tm, d), BF16),
            pltpu.VMEM((2, tm + SUBLANES, FFN_COL_TILE), F32),
            pltpu.VMEM((SUBLANES, 2 * D_FF), F32),
            pltpu.VMEM((tm, D_FF), BF16),
        ],
        compiler_params=_cparams(("arbitrary", "arbitrary")),
        name="conv_ffn",
    )(h1, pre, post, w_up, conv_w, conv_b, w_down)


def _head_constants():
    lane = np.arange(MXU_WIDTH)
    e_mat = (lane[:, None] // RWKV_HEAD == lane[None, :] // RWKV_HEAD).astype(np.float32)
    eye = (np.arange(RWKV_DIM)[None, :] % RWKV_HEAD == np.arange(RWKV_HEAD)[:, None])
    return jnp.asarray(e_mat, BF16), jnp.asarray(eye.astype(np.float32))


def kernel(x, attn_pre_norm, w_in, hgrn_lb, hgrn_gnorm, w_branch_a, rwkv_mu, rwkv_w0, rwkv_w2,
           rwkv_a0, rwkv_a2, rwkv_g2, rwkv_k_k, rwkv_k_a, rwkv_r_k, rwkv_ln_w, rwkv_ln_b,
           w_branch_b, w_out, attn_post_norm, ffn_pre_norm, w_up, conv_w, conv_b, w_down,
           ffn_post_norm):
    bsz, s_len, d = x.shape
    m = bsz * s_len
    e_mat, eye = _head_constants()
    h = x
    for l in range(w_in.shape[0]):
        x2 = h.reshape(m, d)
        pre = attn_pre_norm[l].reshape(1, d)
        w_l = w_in[l].astype(BF16)
        z_h = _inproj(x2, pre, w_l[:, :HGRN_COLS], act=None, out_dtype=F32, tm=256,
                      name="inproj_hgrn")
        z_r = _inproj(x2, pre, w_l[:, HGRN_COLS:HGRN_COLS + RWKV_COLS], act=None,
                      out_dtype=F32, tm=256, name="inproj_rwkv")
        gates = _inproj(x2, pre, w_l[:, HGRN_COLS + RWKV_COLS:], act="sigmoid",
                        out_dtype=BF16, tm=512, name="inproj_gates")

        o_a = _hgrn(z_h.reshape(bsz, s_len, HGRN_COLS), hgrn_lb, hgrn_gnorm[l].reshape(1, d),
                    l, BF16)

        wa2 = jnp.zeros((W_LORA + A_LORA, 2 * d), F32)
        wa2 = wa2.at[:W_LORA, :d].set(rwkv_w2[l]).at[W_LORA:, d:].set(rwkv_a2[l]).astype(BF16)
        r, w, k, v, kk, kb, g, bonus = _rwkv_prep(
            z_r.reshape(bsz, s_len, RWKV_COLS), rwkv_mu[l].reshape(1, -1),
            rwkv_w0[l].reshape(1, d), rwkv_a0[l].reshape(1, d), rwkv_k_k[l].reshape(1, d),
            rwkv_k_a[l].reshape(1, d), rwkv_r_k[l].reshape(1, d), wa2,
            rwkv_g2[l].astype(BF16), e_mat, tm=256)
        y = _rwkv_scan(r, w, k, v, kk, kb, e_mat, eye, tb=64)

        h1 = _merge(x2, o_a.reshape(m, d), y.reshape(m, d), bonus.reshape(m, d),
                    g.reshape(m, d), gates, rwkv_ln_w[l].reshape(1, d),
                    rwkv_ln_b[l].reshape(1, d), attn_post_norm[l].reshape(1, d),
                    w_branch_a[l].astype(BF16), w_branch_b[l].astype(BF16),
                    w_out[l].astype(BF16), e_mat, tm=256)

        h = _ffn(h1.reshape(bsz, s_len, d), ffn_pre_norm[l].reshape(1, d),
                 ffn_post_norm[l].reshape(1, d), w_up[l].astype(BF16), conv_w[l],
                 conv_b[l].reshape(1, -1), w_down[l].astype(BF16), tm=256)
    return h
```

```python
import functools

import jax
import jax.numpy as jnp
import numpy as np
from jax import lax
from jax.experimental import pallas as pl
from jax.experimental.pallas import tpu as pltpu

F32 = jnp.float32
BF16 = jnp.bfloat16

D_MODEL = 1024
HGRN_HEADS = 8
HGRN_EXPAND = 128
HGRN_FWD = HGRN_HEADS * HGRN_EXPAND
HGRN_IN = D_MODEL
HGRN_SCALE = HGRN_EXPAND ** -0.5
CHUNK = 32
RWKV_HEAD = 64
RWKV_DIM = D_MODEL
W_LORA = 64
A_LORA = 64
G_LORA = 128
GN_EPS = 1e-5 * RWKV_HEAD
D_FF = 2816
CONV_W = 3
EPS = 1e-6
HGRN_COLS = 2 * HGRN_FWD + 2 * HGRN_IN
RWKV_COLS = 3 * RWKV_DIM + W_LORA + A_LORA + G_LORA
GATE_COLS = 2 * D_MODEL

SUBLANES = 8
LANES = 128
MXU_WIDTH = 256
VMEM_LIMIT_BYTES = 56 * 1024 * 1024

HGRN_CHUNK_GROUP = 8
FFN_COL_TILE = 256


def _sigmoid(x):
    return 1.0 / (1.0 + jnp.exp(-x))


def _silu(x):
    return x * _sigmoid(x)


def _bdot(a, b):
    return jnp.dot(a.astype(BF16), b.astype(BF16), preferred_element_type=F32)


def _segment_sum_bcast(x, e_ref):
    n = x.shape[-1]
    e = e_ref[...]
    outs = []
    for j in range(n // MXU_WIDTH):
        outs.append(jnp.dot(x[:, j * MXU_WIDTH:(j + 1) * MXU_WIDTH].astype(BF16), e,
                            preferred_element_type=F32))
    return jnp.concatenate(outs, axis=-1)


def _cparams(semantics):
    return pltpu.CompilerParams(dimension_semantics=semantics,
                                vmem_limit_bytes=VMEM_LIMIT_BYTES)


def _inproj_kernel(x_ref, g_ref, w_ref, o_ref, *, act):
    x = x_ref[...]
    ms = jnp.mean(x * x, axis=-1, keepdims=True)
    xn = x * lax.rsqrt(ms + EPS) * g_ref[...]
    z = jnp.dot(xn.astype(BF16), w_ref[...], preferred_element_type=F32)
    if act == "sigmoid":
        z = _sigmoid(z)
    o_ref[...] = z.astype(o_ref.dtype)


def _inproj(x2, gain, w, *, act, out_dtype, tm, name):
    m, d = x2.shape
    n = w.shape[1]
    return pl.pallas_call(
        functools.partial(_inproj_kernel, act=act),
        out_shape=jax.ShapeDtypeStruct((m, n), out_dtype),
        grid=(m // tm,),
        in_specs=[
            pl.BlockSpec((tm, d), lambda i: (i, 0)),
            pl.BlockSpec((1, d), lambda i: (0, 0)),
            pl.BlockSpec((d, n), lambda i: (0, 0)),
        ],
        out_specs=pl.BlockSpec((tm, n), lambda i: (i, 0)),
        compiler_params=_cparams(("arbitrary",)),
        name=name,
    )(x2, gain, w)


def _hgrn_kernel(hq_ref, hf_ref, hi_ref, hg_ref, lb_ref, gn_ref, o_ref, state_ref, *, layer):
    s_len = hq_ref.shape[1]
    rows = HGRN_CHUNK_GROUP * CHUNK
    n_groups = s_len // rows

    lbp = lb_ref[...]
    lbm = jnp.max(lbp, axis=0, keepdims=True)
    lbe = jnp.exp(lbp - lbm)
    lb = (jnp.sum(lbe[0:layer + 1, :], axis=0, keepdims=True)
          / jnp.sum(lbe, axis=0, keepdims=True))
    gn = gn_ref[...]

    state_ref[...] = jnp.zeros_like(state_ref)

    row_in_chunk = lax.broadcasted_iota(jnp.int32, (rows, HGRN_EXPAND), 0) % CHUNK
    ci = lax.broadcasted_iota(jnp.int32, (CHUNK, CHUNK), 0)
    di = lax.broadcasted_iota(jnp.int32, (CHUNK, CHUNK), 1)
    tril = ci >= di

    def group(gi, carry):
        r0 = pl.multiple_of(gi * rows, rows)
        hq = hq_ref[0, pl.ds(r0, rows), :]
        hf = hf_ref[0, pl.ds(r0, rows), :]
        v = hi_ref[0, pl.ds(r0, rows), :]
        hg = hg_ref[0, pl.ds(r0, rows), :]
        q = _silu(hq) * HGRN_SCALE
        f = lb + (1.0 - lb) * _sigmoid(hf)
        k = 1.0 - f
        b = jnp.log(f)
        sh = 1
        while sh < CHUNK:
            b = b + jnp.where(row_in_chunk >= sh, pltpu.roll(b, sh, axis=0), 0.0)
            sh *= 2
        outs = []
        st = state_ref[...]
        for j in range(HGRN_CHUNK_GROUP):
            sl = slice(j * CHUNK, (j + 1) * CHUNK)
            bj, qj, kj, vj = b[sl], q[sl], k[sl], v[sl]
            b_mid = bj[CHUNK // 2 - 1:CHUNK // 2, :]
            b_last = bj[CHUNK - 1:CHUNK, :]
            q_in = qj * jnp.exp(bj - b_mid)
            k_in = kj * jnp.exp(b_mid - bj)
            scores = lax.dot_general(q_in.astype(BF16), k_in.astype(BF16),
                                     (((1,), (1,)), ((), ())), preferred_element_type=F32)
            scores = jnp.where(tril, scores, 0.0)
            o_intra = _bdot(scores, vj)
            q_dec = qj * jnp.exp(bj)
            o_inter = lax.dot_general(q_dec.astype(BF16), st.astype(BF16),
                                      (((1,), (1,)), ((), ())), preferred_element_type=F32)
            k_dec = kj * jnp.exp(b_last - bj)
            u_t = _bdot(vj.T, k_dec)
            st = st * jnp.exp(b_last) + u_t
            outs.append(o_intra + o_inter)
        state_ref[...] = st
        o = jnp.concatenate(outs, axis=0)
        o = o * lax.rsqrt(jnp.mean(o * o, axis=-1, keepdims=True) + EPS)
        o = o * gn * _silu(hg)
        o_ref[0, pl.ds(r0, rows), :] = o.astype(o_ref.dtype)
        return carry

    lax.fori_loop(0, n_groups, group, 0)


def _hgrn(z_h, hgrn_lb, gnorm, layer, out_dtype):
    bsz, s_len, _ = z_h.shape
    nb = HGRN_FWD // HGRN_EXPAND

    def col(off):
        return pl.BlockSpec((1, s_len, HGRN_EXPAND), lambda b, h: (b, 0, off + h))

    return pl.pallas_call(
        functools.partial(_hgrn_kernel, layer=layer),
        out_shape=jax.ShapeDtypeStruct((bsz, s_len, HGRN_IN), out_dtype),
        grid=(bsz, HGRN_HEADS),
        in_specs=[
            col(0), col(nb), col(2 * nb), col(3 * nb),
            pl.BlockSpec((hgrn_lb.shape[0], HGRN_EXPAND), lambda b, h: (0, h)),
            pl.BlockSpec((1, HGRN_EXPAND), lambda b, h: (0, h)),
        ],
        out_specs=pl.BlockSpec((1, s_len, HGRN_EXPAND), lambda b, h: (b, 0, h)),
        scratch_shapes=[pltpu.VMEM((HGRN_EXPAND, HGRN_EXPAND), F32)],
        compiler_params=_cparams(("arbitrary", "arbitrary")),
        name="hgrn2_chunkwise",
    )(z_h, z_h, z_h, z_h, hgrn_lb, gnorm)


def _rwkv_prep_kernel(z_ref, mu_ref, w0_ref, a0_ref, kk_ref, ka_ref, rk_ref, wa2_ref, g2_ref,
                      e_ref, r_out, w_out, k_out, v_out, kk_out, kb_out, g_out, bonus_out,
                      carry_ref):
    tm = z_ref.shape[1]

    @pl.when(pl.program_id(1) == 0)
    def _():
        carry_ref[...] = jnp.zeros_like(carry_ref)

    z = z_ref[0]
    prev = carry_ref[0:1, :]
    rows = lax.broadcasted_iota(jnp.int32, z.shape, 0)
    zs = jnp.where(rows == 0, prev, pltpu.roll(z, 1, axis=0))
    carry_ref[0:1, :] = z[tm - 1:tm, :]
    zm = z + mu_ref[...] * (zs - z)

    d = RWKV_DIM
    rr = zm[:, 0:d]
    kr = zm[:, d:2 * d]
    vr = zm[:, 2 * d:3 * d]
    wa = zm[:, 3 * d:3 * d + W_LORA + A_LORA]
    gz = zm[:, 3 * d + W_LORA + A_LORA:]
    lane = lax.broadcasted_iota(jnp.int32, wa.shape, 1)
    wa = jnp.where(lane < W_LORA, jnp.tanh(wa), wa)
    wa_out = _bdot(wa, wa2_ref[...])
    w_pre = w0_ref[...] + wa_out[:, 0:d]
    a_pre = a0_ref[...] + wa_out[:, d:2 * d]
    log_decay = -float(np.exp(-0.5)) * _sigmoid(w_pre)
    a = _sigmoid(a_pre)
    g = _bdot(_sigmoid(gz), g2_ref[...])

    kk = kr * kk_ref[...]
    ss = _segment_sum_bcast(kk * kk, e_ref)
    kk = kk / jnp.maximum(jnp.sqrt(ss), 1e-12)
    k2 = kr * (1.0 + (a - 1.0) * ka_ref[...])
    bonus = _segment_sum_bcast(rr * k2 * rk_ref[...], e_ref) * vr

    r_out[0] = rr
    w_out[0] = log_decay
    k_out[0] = k2
    v_out[0] = vr
    kk_out[0] = kk
    kb_out[0] = kk * a
    g_out[0] = g
    bonus_out[0] = bonus


def _rwkv_prep(z_r, mu, w0, a0, k_k, k_a, r_k, wa2, g2, e_mat, tm):
    bsz, s_len, _ = z_r.shape
    d = RWKV_DIM
    row = lambda n: pl.BlockSpec((1, n), lambda b, i: (0, 0))
    out_spec = pl.BlockSpec((1, tm, d), lambda b, i: (b, i, 0))
    out_shape = jax.ShapeDtypeStruct((bsz, s_len, d), F32)
    return pl.pallas_call(
        _rwkv_prep_kernel,
        out_shape=[out_shape] * 8,
        grid=(bsz, s_len // tm),
        in_specs=[
            pl.BlockSpec((1, tm, RWKV_COLS), lambda b, i: (b, i, 0)),
            row(RWKV_COLS), row(d), row(d), row(d), row(d), row(d),
            pl.BlockSpec((W_LORA + A_LORA, 2 * d), lambda b, i: (0, 0)),
            pl.BlockSpec((G_LORA, d), lambda b, i: (0, 0)),
            pl.BlockSpec((MXU_WIDTH, MXU_WIDTH), lambda b, i: (0, 0)),
        ],
        out_specs=[out_spec] * 8,
        scratch_shapes=[pltpu.VMEM((SUBLANES, RWKV_COLS), F32)],
        compiler_params=_cparams(("arbitrary", "arbitrary")),
        name="rwkv7_prep",
    )(z_r, mu, w0, a0, k_k, k_a, r_k, wa2, g2, e_mat)


RWKV_CHUNK = 64
HEADS_PER_TILE = MXU_WIDTH // RWKV_HEAD
RWKV_BATCH_UNROLL = 4


def _rwkv_chunk_kernel(r_ref, lw_ref, k_ref, v_ref, kk_ref, kb_ref, y_ref, s_ref):
    bsz = r_ref.shape[0]
    c = RWKV_CHUNK
    n_tiles = r_ref.shape[2] // MXU_WIDTH

    @pl.when(pl.program_id(0) == 0)
    def _():
        s_ref[...] = jnp.zeros_like(s_ref)

    t_idx = lax.broadcasted_iota(jnp.int32, (c, MXU_WIDTH), 0)
    lane = lax.broadcasted_iota(jnp.int32, (c, MXU_WIDTH), 1)
    s_idx = lane % c
    head_masks = [lane // RWKV_HEAD == h for h in range(HEADS_PER_TILE)]
    strict = s_idx < t_idx
    incl = s_idx <= t_idx
    eye = (s_idx == t_idx).astype(F32)

    def same_block(size):
        return (t_idx // size) == (s_idx // size)

    def bd(x):
        xb = x.astype(BF16)
        zero = jnp.zeros_like(xb)
        return jnp.concatenate([jnp.where(m, xb, zero) for m in head_masks], axis=0)

    def pk(a, b_bd):
        return jnp.dot(a.astype(BF16), b_bd, preferred_element_type=F32)

    def pk_t(a, b_bd):
        return lax.dot_general(a.astype(BF16), b_bd, (((1,), (1,)), ((), ())),
                               preferred_element_type=F32)

    def tile(r, lw, k, v, kk, kb, s0):
        g = lw
        sh = 1
        while sh < c:
            g = g + jnp.where(t_idx >= sh, pltpu.roll(g, sh, axis=0), 0.0)
            sh *= 2
        g_mid = g[c // 2 - 1:c // 2, :]
        g_last = g[c - 1:c, :]
        e_neg = jnp.exp(g_mid - g)
        a_t = -kk * jnp.exp(g - lw - g_mid)
        r_t = r * jnp.exp(g - g_mid)
        b_t = kb * e_neg
        k_t = k * e_neg
        e_last = jnp.exp(g_last - g_mid)

        ar = jnp.concatenate([a_t, r_t], axis=0)
        nb = pk_t(ar, bd(b_t))
        nk = pk_t(ar, bd(k_t))
        yield
        n_ab = jnp.where(strict, nb[0:c], 0.0)
        m_rb = jnp.where(incl, nb[c:], 0.0)
        n_ak = jnp.where(strict, nk[0:c], 0.0)
        m_rk = jnp.where(incl, nk[c:], 0.0)

        n8 = jnp.where(same_block(8), n_ab, 0.0)
        t_inv = eye + n8
        n_pow = pk(n8, bd(n8))
        qy = pk(jnp.concatenate([n_ak, m_rk], axis=0), bd(v))
        yield
        t_inv = t_inv + pk(n_pow, bd(t_inv))
        n_pow = pk(n_pow, bd(n_pow))
        yield
        t_inv = t_inv + pk(n_pow, bd(t_inv))
        yield
        for size in (16, 32, 64):
            off = jnp.where(same_block(size) & jnp.logical_not(same_block(size // 2)), n_ab, 0.0)
            z = pk(t_inv, bd(off))
            yield
            t_inv = t_inv + pk(z, bd(t_inv))
            yield

        w_mat = pk(t_inv, bd(a_t))
        u0 = pk(t_inv, bd(qy[0:c]))
        yield
        x = pk_t(jnp.concatenate([w_mat, r_t], axis=0), bd(s0 * jnp.exp(g_mid)))
        yield
        p = x[0:c] + u0
        y = x[c:] + qy[c:] + pk(m_rb, bd(p))
        upd = lax.dot_general(jnp.concatenate([p, v], axis=0).astype(BF16),
                              jnp.concatenate([b_t * e_last, k_t * e_last], axis=0).astype(BF16),
                              (((0,), (0,)), ((), ())), preferred_element_type=F32)
        yield
        s_new = s0 * jnp.exp(g_last)
        for h in range(HEADS_PER_TILE):
            s_new = s_new + jnp.where(head_masks[h], upd[h * RWKV_HEAD:(h + 1) * RWKV_HEAD], 0.0)
        return y, s_new

    def run_interleaved(gens):
        results = [None] * len(gens)
        live = list(range(len(gens)))
        while live:
            for idx in list(live):
                try:
                    next(gens[idx])
                except StopIteration as stop:
                    results[idx] = stop.value
                    live.remove(idx)
        return results

    def batch_group(i, carry):
        where = [(i * RWKV_BATCH_UNROLL + bb, slice(j * MXU_WIDTH, (j + 1) * MXU_WIDTH))
                 for bb in range(RWKV_BATCH_UNROLL) for j in range(n_tiles)]
        operands = [[ref[b, :, cols] for ref in (r_ref, lw_ref, k_ref, v_ref, kk_ref, kb_ref, s_ref)]
                    for b, cols in where]
        results = run_interleaved([tile(*ops) for ops in operands])
        for (b, cols), (y, s_new) in zip(where, results):
            y_ref[b, :, cols] = y
            s_ref[b, :, cols] = s_new
        return carry

    lax.fori_loop(0, bsz // RWKV_BATCH_UNROLL, batch_group, 0)


def _rwkv_chunk(r, lw, k, v, kk, kb):
    bsz, s_len, d = r.shape
    blk = pl.BlockSpec((bsz, RWKV_CHUNK, d), lambda i: (0, i, 0))
    return pl.pallas_call(
        _rwkv_chunk_kernel,
        out_shape=jax.ShapeDtypeStruct((bsz, s_len, d), F32),
        grid=(s_len // RWKV_CHUNK,),
        in_specs=[blk] * 6,
        out_specs=blk,
        scratch_shapes=[pltpu.VMEM((bsz, RWKV_HEAD, d), F32)],
        compiler_params=_cparams(("arbitrary",)),
        name="rwkv7_chunk",
    )(r, lw, k, v, kk, kb)


def _merge_kernel(x_ref, oa_ref, y_ref, bonus_ref, g_ref, ga_ref, gb_ref, lnw_ref, lnb_ref,
                  pn_ref, wa_ref, wb_ref, wo_ref, e_ref, o_ref):
    y = y_ref[...]
    inv_n = 1.0 / RWKV_HEAD
    mu = _segment_sum_bcast(y, e_ref) * inv_n
    yc = y - mu
    var = _segment_sum_bcast(yc * yc, e_ref) * inv_n
    yn = yc * lax.rsqrt(var + GN_EPS) * lnw_ref[...] + lnb_ref[...]
    o_b = (yn + bonus_ref[...]) * g_ref[...]
    y_b = jnp.dot(o_b.astype(BF16), wb_ref[...], preferred_element_type=F32)
    y_a = jnp.dot(oa_ref[...].astype(BF16), wa_ref[...], preferred_element_type=F32)
    merged = ga_ref[...].astype(F32) * y_a + gb_ref[...].astype(F32) * y_b
    mix = jnp.dot(merged.astype(BF16), wo_ref[...], preferred_element_type=F32)
    ms = jnp.mean(mix * mix, axis=-1, keepdims=True)
    o_ref[...] = x_ref[...] + mix * lax.rsqrt(ms + EPS) * pn_ref[...]


def _merge(x2, o_a, y, bonus, g, gates, ln_w, ln_b, post_norm, w_a, w_b, w_o, e_mat, tm):
    m, d = x2.shape
    blk = pl.BlockSpec((tm, d), lambda i: (i, 0))
    row = pl.BlockSpec((1, d), lambda i: (0, 0))
    wsp = pl.BlockSpec((d, d), lambda i: (0, 0))
    return pl.pallas_call(
        _merge_kernel,
        out_shape=jax.ShapeDtypeStruct((m, d), F32),
        grid=(m // tm,),
        in_specs=[blk, blk, blk, blk, blk,
                  pl.BlockSpec((tm, d), lambda i: (i, 0)),
                  pl.BlockSpec((tm, d), lambda i: (i, 1)),
                  row, row, row, wsp, wsp, wsp,
                  pl.BlockSpec((MXU_WIDTH, MXU_WIDTH), lambda i: (0, 0))],
        out_specs=blk,
        compiler_params=_cparams(("arbitrary",)),
        name="merge_outproj",
    )(x2, o_a, y, bonus, g, gates, gates, ln_w, ln_b, post_norm, w_a, w_b, w_o, e_mat)


def _ffn_kernel(h_ref, pre_ref, post_ref, wup_ref, cw_ref, cb_ref, wdn_ref, o_ref,
                xn_ref, buf_ref, carry_ref, act_ref):
    tm = h_ref.shape[1]
    ct = FFN_COL_TILE
    n_tiles = D_FF // ct
    halo = SUBLANES

    @pl.when(pl.program_id(1) == 0)
    def _():
        carry_ref[...] = jnp.zeros_like(carry_ref)

    h = h_ref[0]
    ms = jnp.mean(h * h, axis=-1, keepdims=True)
    xn_ref[...] = (h * lax.rsqrt(ms + EPS) * pre_ref[...]).astype(BF16)

    def conv(c0, slot):
        cols = slice(c0, c0 + ct)
        hu = jnp.dot(xn_ref[...], wup_ref[:, cols], preferred_element_type=F32)
        buf_ref[slot, 0:halo, :] = carry_ref[:, cols]
        buf_ref[slot, halo:halo + tm, :] = hu
        carry_ref[:, cols] = hu[tm - halo:tm, :]
        cw = cw_ref[:, cols]
        out = cb_ref[:, cols] + cw[CONV_W - 1:CONV_W, :] * hu
        for j in range(CONV_W - 1):
            back = CONV_W - 1 - j
            out = out + cw[j:j + 1, :] * buf_ref[slot, halo - back:halo - back + tm, :]
        return out

    for i in range(n_tiles):
        gate = conv(i * ct, 0)
        val = conv(D_FF + i * ct, 1)
        act_ref[:, i * ct:(i + 1) * ct] = (_silu(gate) * val).astype(BF16)

    ff = jnp.dot(act_ref[...], wdn_ref[...], preferred_element_type=F32)
    ms2 = jnp.mean(ff * ff, axis=-1, keepdims=True)
    o_ref[0] = h + ff * lax.rsqrt(ms2 + EPS) * post_ref[...]


def _ffn(h1, pre, post, w_up, conv_w, conv_b, w_down, tm):
    bsz, s_len, d = h1.shape
    blk = pl.BlockSpec((1, tm, d), lambda b, i: (b, i, 0))
    row = pl.BlockSpec((1, d), lambda b, i: (0, 0))
    return pl.pallas_call(
        _ffn_kernel,
        out_shape=jax.ShapeDtypeStruct((bsz, s_len, d), F32),
        grid=(bsz, s_len // tm),
        in_specs=[blk, row, row,
                  pl.BlockSpec((d, 2 * D_FF), lambda b, i: (0, 0)),
                  pl.BlockSpec((CONV_W, 2 * D_FF), lambda b, i: (0, 0)),
                  pl.BlockSpec((1, 2 * D_FF), lambda b, i: (0, 0)),
                  pl.BlockSpec((D_FF, d), lambda b, i: (0, 0))],
        out_specs=blk,
        scratch_shapes=[
            pltpu.VMEM((tm, d), BF16),
            pltpu.VMEM((2, tm + SUBLANES, FFN_COL_TILE), F32),
            pltpu.VMEM((SUBLANES, 2 * D_FF), F32),
            pltpu.VMEM((tm, D_FF), BF16),
        ],
        compiler_params=_cparams(("arbitrary", "arbitrary")),
        name="conv_ffn",
    )(h1, pre, post, w_up, conv_w, conv_b, w_down)


def _head_segment_ones():
    lane = np.arange(MXU_WIDTH)
    e_mat = (lane[:, None] // RWKV_HEAD == lane[None, :] // RWKV_HEAD).astype(np.float32)
    return jnp.asarray(e_mat, BF16)


def kernel(x, attn_pre_norm, w_in, hgrn_lb, hgrn_gnorm, w_branch_a, rwkv_mu, rwkv_w0, rwkv_w2,
           rwkv_a0, rwkv_a2, rwkv_g2, rwkv_k_k, rwkv_k_a, rwkv_r_k, rwkv_ln_w, rwkv_ln_b,
           w_branch_b, w_out, attn_post_norm, ffn_pre_norm, w_up, conv_w, conv_b, w_down,
           ffn_post_norm):
    bsz, s_len, d = x.shape
    m = bsz * s_len
    e_mat = _head_segment_ones()
    h = x
    for l in range(w_in.shape[0]):
        x2 = h.reshape(m, d)
        pre = attn_pre_norm[l].reshape(1, d)
        w_l = w_in[l].astype(BF16)
        z_h = _inproj(x2, pre, w_l[:, :HGRN_COLS], act=None, out_dtype=F32, tm=256,
                      name="inproj_hgrn")
        z_r = _inproj(x2, pre, w_l[:, HGRN_COLS:HGRN_COLS + RWKV_COLS], act=None,
                      out_dtype=F32, tm=256, name="inproj_rwkv")
        gates = _inproj(x2, pre, w_l[:, HGRN_COLS + RWKV_COLS:], act="sigmoid",
                        out_dtype=BF16, tm=512, name="inproj_gates")

        o_a = _hgrn(z_h.reshape(bsz, s_len, HGRN_COLS), hgrn_lb, hgrn_gnorm[l].reshape(1, d),
                    l, BF16)

        wa2 = jnp.zeros((W_LORA + A_LORA, 2 * d), F32)
        wa2 = wa2.at[:W_LORA, :d].set(rwkv_w2[l]).at[W_LORA:, d:].set(rwkv_a2[l]).astype(BF16)
        r, lw, k, v, kk, kb, g, bonus = _rwkv_prep(
            z_r.reshape(bsz, s_len, RWKV_COLS), rwkv_mu[l].reshape(1, -1),
            rwkv_w0[l].reshape(1, d), rwkv_a0[l].reshape(1, d), rwkv_k_k[l].reshape(1, d),
            rwkv_k_a[l].reshape(1, d), rwkv_r_k[l].reshape(1, d), wa2,
            rwkv_g2[l].astype(BF16), e_mat, tm=256)
        y = _rwkv_chunk(r, lw, k, v, kk, kb)

        h1 = _merge(x2, o_a.reshape(m, d), y.reshape(m, d), bonus.reshape(m, d),
                    g.reshape(m, d), gates, rwkv_ln_w[l].reshape(1, d),
                    rwkv_ln_b[l].reshape(1, d), attn_post_norm[l].reshape(1, d),
                    w_branch_a[l].astype(BF16), w_branch_b[l].astype(BF16),
                    w_out[l].astype(BF16), e_mat, tm=256)

        h = _ffn(h1.reshape(bsz, s_len, d), ffn_pre_norm[l].reshape(1, d),
                 ffn_post_norm[l].reshape(1, d), w_up[l].astype(BF16), conv_w[l],
                 conv_b[l].reshape(1, -1), w_down[l].astype(BF16), tm=256)
    return h
```

```python
import functools

import jax
import jax.numpy as jnp
import numpy as np
from jax import lax
from jax.experimental import pallas as pl
from jax.experimental.pallas import tpu as pltpu

F32 = jnp.float32
BF16 = jnp.bfloat16

D_MODEL = 1024
HGRN_HEADS = 8
HGRN_EXPAND = 128
HGRN_FWD = HGRN_HEADS * HGRN_EXPAND
HGRN_IN = D_MODEL
HGRN_SCALE = HGRN_EXPAND ** -0.5
CHUNK = 32
RWKV_HEAD = 64
RWKV_DIM = D_MODEL
W_LORA = 64
A_LORA = 64
G_LORA = 128
GN_EPS = 1e-5 * RWKV_HEAD
D_FF = 2816
CONV_W = 3
EPS = 1e-6
HGRN_COLS = 2 * HGRN_FWD + 2 * HGRN_IN
RWKV_COLS = 3 * RWKV_DIM + W_LORA + A_LORA + G_LORA
GATE_COLS = 2 * D_MODEL

SUBLANES = 8
LANES = 128
MXU_WIDTH = 256
VMEM_LIMIT_BYTES = 56 * 1024 * 1024

HGRN_CHUNK_GROUP = 8
FFN_COL_TILE = 256


def _sigmoid(x):
    return 1.0 / (1.0 + jnp.exp(-x))


def _silu(x):
    return x * _sigmoid(x)


def _bdot(a, b):
    return jnp.dot(a.astype(BF16), b.astype(BF16), preferred_element_type=F32)


def _segment_sum_bcast(x, e_ref):
    n = x.shape[-1]
    e = e_ref[...]
    outs = []
    for j in range(n // MXU_WIDTH):
        outs.append(jnp.dot(x[:, j * MXU_WIDTH:(j + 1) * MXU_WIDTH].astype(BF16), e,
                            preferred_element_type=F32))
    return jnp.concatenate(outs, axis=-1)


def _cparams(semantics):
    return pltpu.CompilerParams(dimension_semantics=semantics,
                                vmem_limit_bytes=VMEM_LIMIT_BYTES)


def _inproj_kernel(x_ref, g_ref, w_ref, *o_refs, act):
    x = x_ref[...]
    ms = jnp.mean(x * x, axis=-1, keepdims=True)
    xn = x * lax.rsqrt(ms + EPS) * g_ref[...]
    z = jnp.dot(xn.astype(BF16), w_ref[...], preferred_element_type=F32)
    if act == "sigmoid":
        z = _sigmoid(z)
    off = 0
    for o_ref in o_refs:
        n = o_ref.shape[1]
        o_ref[...] = z[:, off:off + n].astype(o_ref.dtype)
        off += n


def _inproj(x2, gain, w, *, sections, act, tm, name):
    m, d = x2.shape
    n = w.shape[1]
    assert sum(cols for cols, _ in sections) == n
    return pl.pallas_call(
        functools.partial(_inproj_kernel, act=act),
        out_shape=[jax.ShapeDtypeStruct((m, cols), dt) for cols, dt in sections],
        grid=(m // tm,),
        in_specs=[
            pl.BlockSpec((tm, d), lambda i: (i, 0)),
            pl.BlockSpec((1, d), lambda i: (0, 0)),
            pl.BlockSpec((d, n), lambda i: (0, 0)),
        ],
        out_specs=[pl.BlockSpec((tm, cols), lambda i: (i, 0)) for cols, _ in sections],
        compiler_params=_cparams(("arbitrary",)),
        name=name,
    )(x2, gain, w)


def _hgrn_kernel(hq_ref, hf_ref, hi_ref, hg_ref, lb_ref, gn_ref, o_ref, state_ref, *, layer):
    s_len = hq_ref.shape[1]
    rows = HGRN_CHUNK_GROUP * CHUNK
    n_groups = s_len // rows

    lbp = lb_ref[...]
    lbm = jnp.max(lbp, axis=0, keepdims=True)
    lbe = jnp.exp(lbp - lbm)
    lb = (jnp.sum(lbe[0:layer + 1, :], axis=0, keepdims=True)
          / jnp.sum(lbe, axis=0, keepdims=True))
    gn = gn_ref[...]

    state_ref[...] = jnp.zeros_like(state_ref)

    row_in_chunk = lax.broadcasted_iota(jnp.int32, (rows, HGRN_EXPAND), 0) % CHUNK
    ci = lax.broadcasted_iota(jnp.int32, (CHUNK, CHUNK), 0)
    di = lax.broadcasted_iota(jnp.int32, (CHUNK, CHUNK), 1)
    tril = ci >= di

    def group(gi, carry):
        r0 = pl.multiple_of(gi * rows, rows)
        hq = hq_ref[0, pl.ds(r0, rows), :].astype(F32)
        hf = hf_ref[0, pl.ds(r0, rows), :]
        v = hi_ref[0, pl.ds(r0, rows), :].astype(F32)
        hg = hg_ref[0, pl.ds(r0, rows), :].astype(F32)
        q = _silu(hq) * HGRN_SCALE
        f = lb + (1.0 - lb) * _sigmoid(hf)
        k = 1.0 - f
        b = jnp.log(f)
        sh = 1
        while sh < CHUNK:
            b = b + jnp.where(row_in_chunk >= sh, pltpu.roll(b, sh, axis=0), 0.0)
            sh *= 2
        chunks = []
        for j in range(HGRN_CHUNK_GROUP):
            sl = slice(j * CHUNK, (j + 1) * CHUNK)
            bj, qj, kj, vj = b[sl], q[sl], k[sl], v[sl]
            b_mid = bj[CHUNK // 2 - 1:CHUNK // 2, :]
            b_last = bj[CHUNK - 1:CHUNK, :]
            q_in = qj * jnp.exp(bj - b_mid)
            k_in = kj * jnp.exp(b_mid - bj)
            scores = lax.dot_general(q_in.astype(BF16), k_in.astype(BF16),
                                     (((1,), (1,)), ((), ())), preferred_element_type=F32)
            k_dec = kj * jnp.exp(b_last - bj)
            u_t = _bdot(vj.T, k_dec)
            chunks.append((scores, u_t, qj * jnp.exp(bj), jnp.exp(b_last), vj))
        o_intra = [_bdot(jnp.where(tril, scores, 0.0), vj) for scores, _, _, _, vj in chunks]
        outs = []
        st = state_ref[...]
        for j, (_, u_t, q_dec, decay, _) in enumerate(chunks):
            o_inter = lax.dot_general(q_dec.astype(BF16), st.astype(BF16),
                                      (((1,), (1,)), ((), ())), preferred_element_type=F32)
            st = st * decay + u_t
            outs.append(o_intra[j] + o_inter)
        state_ref[...] = st
        o = jnp.concatenate(outs, axis=0)
        o = o * lax.rsqrt(jnp.mean(o * o, axis=-1, keepdims=True) + EPS)
        o = o * gn * _silu(hg)
        o_ref[0, pl.ds(r0, rows), :] = o.astype(o_ref.dtype)
        return carry

    lax.fori_loop(0, n_groups, group, 0, unroll=4)


def _hgrn(zq, zf, zi, zg, hgrn_lb, gnorm, layer, out_dtype):
    bsz, s_len, _ = zq.shape
    col = pl.BlockSpec((1, s_len, HGRN_EXPAND), lambda b, h: (b, 0, h))

    return pl.pallas_call(
        functools.partial(_hgrn_kernel, layer=layer),
        out_shape=jax.ShapeDtypeStruct((bsz, s_len, HGRN_IN), out_dtype),
        grid=(bsz, HGRN_HEADS),
        in_specs=[
            col, col, col, col,
            pl.BlockSpec((hgrn_lb.shape[0], HGRN_EXPAND), lambda b, h: (0, h)),
            pl.BlockSpec((1, HGRN_EXPAND), lambda b, h: (0, h)),
        ],
        out_specs=pl.BlockSpec((1, s_len, HGRN_EXPAND), lambda b, h: (b, 0, h)),
        scratch_shapes=[pltpu.VMEM((HGRN_EXPAND, HGRN_EXPAND), F32)],
        compiler_params=_cparams(("arbitrary", "arbitrary")),
        name="hgrn2_chunkwise",
    )(zq, zf, zi, zg, hgrn_lb, gnorm)


RWKV_CHUNK = 64
HEADS_PER_TILE = MXU_WIDTH // RWKV_HEAD
RWKV_BATCH_UNROLL = 4


def _rwkv_prep(z, zs, mu_ref, w0_ref, a0_ref, kkw_ref, ka_ref, rk_ref, wa2_ref, g2_ref, e_ref):
    zm = z + mu_ref[...] * (zs - z)
    d = RWKV_DIM
    rr = zm[:, 0:d]
    kr = zm[:, d:2 * d]
    vr = zm[:, 2 * d:3 * d]
    wa = zm[:, 3 * d:3 * d + W_LORA + A_LORA]
    gz = zm[:, 3 * d + W_LORA + A_LORA:]
    lane = lax.broadcasted_iota(jnp.int32, wa.shape, 1)
    wa = jnp.where(lane < W_LORA, jnp.tanh(wa), wa)
    wa_out = _bdot(wa, wa2_ref[...])
    w_pre = w0_ref[...] + wa_out[:, 0:d]
    a_pre = a0_ref[...] + wa_out[:, d:2 * d]
    log_decay = -float(np.exp(-0.5)) * _sigmoid(w_pre)
    a = _sigmoid(a_pre)
    g = _bdot(_sigmoid(gz), g2_ref[...])
    kk = kr * kkw_ref[...]
    ss = _segment_sum_bcast(kk * kk, e_ref)
    kk = kk / jnp.maximum(jnp.sqrt(ss), 1e-12)
    k2 = kr * (1.0 + (a - 1.0) * ka_ref[...])
    bonus = _segment_sum_bcast(rr * k2 * rk_ref[...], e_ref) * vr
    return rr, log_decay, k2, vr, kk, kk * a, g, bonus


def _rwkv_chunk_kernel(z_ref, mu_ref, w0_ref, a0_ref, kkw_ref, ka_ref, rk_ref, wa2_ref, g2_ref,
                       e_ref, y_ref, g_ref, bonus_ref, s_ref, carry_ref):
    bsz = z_ref.shape[0]
    c = RWKV_CHUNK
    n_tiles = RWKV_DIM // MXU_WIDTH

    @pl.when(pl.program_id(0) == 0)
    def _():
        s_ref[...] = jnp.zeros_like(s_ref)
        carry_ref[...] = jnp.zeros_like(carry_ref)

    first_row = lax.broadcasted_iota(jnp.int32, (c, RWKV_COLS), 0) == 0

    t_idx = lax.broadcasted_iota(jnp.int32, (c, MXU_WIDTH), 0)
    lane = lax.broadcasted_iota(jnp.int32, (c, MXU_WIDTH), 1)
    s_idx = lane % c
    head_masks = [lane // RWKV_HEAD == h for h in range(HEADS_PER_TILE)]
    strict = s_idx < t_idx
    incl = s_idx <= t_idx
    eye = (s_idx == t_idx).astype(F32)

    def same_block(size):
        return (t_idx // size) == (s_idx // size)

    def bd(x):
        xb = x.astype(BF16)
        zero = jnp.zeros_like(xb)
        return jnp.concatenate([jnp.where(m, xb, zero) for m in head_masks], axis=0)

    def pk(a, b_bd):
        return jnp.dot(a.astype(BF16), b_bd, preferred_element_type=F32)

    def pk_t(a, b_bd):
        return lax.dot_general(a.astype(BF16), b_bd, (((1,), (1,)), ((), ())),
                               preferred_element_type=F32)

    def tile(r, lw, k, v, kk, kb, s0):
        g = lw
        sh = 1
        while sh < c:
            g = g + jnp.where(t_idx >= sh, pltpu.roll(g, sh, axis=0), 0.0)
            sh *= 2
        g_mid = g[c // 2 - 1:c // 2, :]
        g_last = g[c - 1:c, :]
        e_neg = jnp.exp(g_mid - g)
        a_t = -kk * jnp.exp(g - lw - g_mid)
        r_t = r * jnp.exp(g - g_mid)
        b_t = kb * e_neg
        k_t = k * e_neg
        e_last = jnp.exp(g_last - g_mid)

        ar = jnp.concatenate([a_t, r_t], axis=0)
        nb = pk_t(ar, bd(b_t))
        nk = pk_t(ar, bd(k_t))
        yield
        n_ab = jnp.where(strict, nb[0:c], 0.0)
        m_rb = jnp.where(incl, nb[c:], 0.0)
        n_ak = jnp.where(strict, nk[0:c], 0.0)
        m_rk = jnp.where(incl, nk[c:], 0.0)

        n8 = jnp.where(same_block(8), n_ab, 0.0)
        t_inv = eye + n8
        n_pow = pk(n8, bd(n8))
        qy = pk(jnp.concatenate([n_ak, m_rk], axis=0), bd(v))
        yield
        t_inv = t_inv + pk(n_pow, bd(t_inv))
        n_pow = pk(n_pow, bd(n_pow))
        yield
        t_inv = t_inv + pk(n_pow, bd(t_inv))
        yield
        for size in (16, 32, 64):
            off = jnp.where(same_block(size) & jnp.logical_not(same_block(size // 2)), n_ab, 0.0)
            z = pk(t_inv, bd(off))
            yield
            t_inv = t_inv + pk(z, bd(t_inv))
            yield

        w_mat = pk(t_inv, bd(a_t))
        u0 = pk(t_inv, bd(qy[0:c]))
        yield
        x = pk_t(jnp.concatenate([w_mat, r_t], axis=0), bd(s0 * jnp.exp(g_mid)))
        yield
        p = x[0:c] + u0
        y = x[c:] + qy[c:] + pk(m_rb, bd(p))
        upd = lax.dot_general(jnp.concatenate([p, v], axis=0).astype(BF16),
                              jnp.concatenate([b_t * e_last, k_t * e_last], axis=0).astype(BF16),
                              (((0,), (0,)), ((), ())), preferred_element_type=F32)
        yield
        s_new = s0 * jnp.exp(g_last)
        for h in range(HEADS_PER_TILE):
            s_new = s_new + jnp.where(head_masks[h], upd[h * RWKV_HEAD:(h + 1) * RWKV_HEAD], 0.0)
        return y, s_new

    def run_interleaved(gens):
        results = [None] * len(gens)
        live = list(range(len(gens)))
        while live:
            for idx in list(live):
                try:
                    next(gens[idx])
                except StopIteration as stop:
                    results[idx] = stop.value
                    live.remove(idx)
        return results

    def batch_group(i, carry):
        batches = [i * RWKV_BATCH_UNROLL + bb for bb in range(RWKV_BATCH_UNROLL)]
        z = [z_ref[b].astype(F32) for b in batches]
        zs = [jnp.where(first_row, carry_ref[b, 0:1, :], pltpu.roll(zb, 1, axis=0))
              for b, zb in zip(batches, z)]
        states = [s_ref[b] for b in batches]
        prep = _rwkv_prep(jnp.concatenate(z, axis=0), jnp.concatenate(zs, axis=0), mu_ref, w0_ref,
                          a0_ref, kkw_ref, ka_ref, rk_ref, wa2_ref, g2_ref, e_ref)
        r, lw, k, v, kk, kb, g, bonus = prep

        gens = []
        for bb in range(RWKV_BATCH_UNROLL):
            rows = slice(bb * c, (bb + 1) * c)
            for j in range(n_tiles):
                cols = slice(j * MXU_WIDTH, (j + 1) * MXU_WIDTH)
                gens.append(tile(*(t[rows, cols] for t in (r, lw, k, v, kk, kb)),
                                 states[bb][:, cols]))
        results = run_interleaved(gens)

        for bb, b in enumerate(batches):
            rows = slice(bb * c, (bb + 1) * c)
            carry_ref[b, 0:1, :] = z[bb][c - 1:c, :]
            g_ref[b] = g[rows].astype(g_ref.dtype)
            bonus_ref[b] = bonus[rows].astype(bonus_ref.dtype)
            for j in range(n_tiles):
                y, s_new = results[bb * n_tiles + j]
                cols = slice(j * MXU_WIDTH, (j + 1) * MXU_WIDTH)
                y_ref[b, :, cols] = y.astype(y_ref.dtype)
                s_ref[b, :, cols] = s_new
        return carry

    lax.fori_loop(0, bsz // RWKV_BATCH_UNROLL, batch_group, 0)


def _rwkv_chunk(z_r, mu, w0, a0, k_k, k_a, r_k, wa2, g2, e_mat):
    bsz, s_len, _ = z_r.shape
    d = RWKV_DIM
    assert bsz % RWKV_BATCH_UNROLL == 0 and s_len % RWKV_CHUNK == 0
    whole = lambda a: pl.BlockSpec(a.shape, lambda i: (0,) * a.ndim)
    out_blk = pl.BlockSpec((bsz, RWKV_CHUNK, d), lambda i: (0, i, 0))
    params = (mu, w0, a0, k_k, k_a, r_k, wa2, g2, e_mat)
    return pl.pallas_call(
        _rwkv_chunk_kernel,
        out_shape=[jax.ShapeDtypeStruct((bsz, s_len, d), BF16)] * 3,
        grid=(s_len // RWKV_CHUNK,),
        in_specs=[pl.BlockSpec((bsz, RWKV_CHUNK, RWKV_COLS), lambda i: (0, i, 0))]
        + [whole(p) for p in params],
        out_specs=[out_blk] * 3,
        scratch_shapes=[pltpu.VMEM((bsz, RWKV_HEAD, d), F32),
                        pltpu.VMEM((bsz, SUBLANES, RWKV_COLS), F32)],
        compiler_params=_cparams(("arbitrary",)),
        name="rwkv7_chunk",
    )(z_r, *params)


def _merge_kernel(x_ref, oa_ref, y_ref, bonus_ref, g_ref, ga_ref, gb_ref, lnw_ref, lnb_ref,
                  pn_ref, wa_ref, wb_ref, wo_ref, e_ref, o_ref):
    y = y_ref[...].astype(F32)
    inv_n = 1.0 / RWKV_HEAD
    mu = _segment_sum_bcast(y, e_ref) * inv_n
    yc = y - mu
    var = _segment_sum_bcast(yc * yc, e_ref) * inv_n
    yn = yc * lax.rsqrt(var + GN_EPS) * lnw_ref[...] + lnb_ref[...]
    o_b = (yn + bonus_ref[...].astype(F32)) * g_ref[...].astype(F32)
    y_b = jnp.dot(o_b.astype(BF16), wb_ref[...], preferred_element_type=F32)
    y_a = jnp.dot(oa_ref[...].astype(BF16), wa_ref[...], preferred_element_type=F32)
    merged = ga_ref[...].astype(F32) * y_a + gb_ref[...].astype(F32) * y_b
    mix = jnp.dot(merged.astype(BF16), wo_ref[...], preferred_element_type=F32)
    ms = jnp.mean(mix * mix, axis=-1, keepdims=True)
    o_ref[...] = x_ref[...] + mix * lax.rsqrt(ms + EPS) * pn_ref[...]


def _merge(x2, o_a, y, bonus, g, gates, ln_w, ln_b, post_norm, w_a, w_b, w_o, e_mat, tm):
    m, d = x2.shape
    blk = pl.BlockSpec((tm, d), lambda i: (i, 0))
    row = pl.BlockSpec((1, d), lambda i: (0, 0))
    wsp = pl.BlockSpec((d, d), lambda i: (0, 0))
    return pl.pallas_call(
        _merge_kernel,
        out_shape=jax.ShapeDtypeStruct((m, d), F32),
        grid=(m // tm,),
        in_specs=[blk, blk, blk, blk, blk,
                  pl.BlockSpec((tm, d), lambda i: (i, 0)),
                  pl.BlockSpec((tm, d), lambda i: (i, 1)),
                  row, row, row, wsp, wsp, wsp,
                  pl.BlockSpec((MXU_WIDTH, MXU_WIDTH), lambda i: (0, 0))],
        out_specs=blk,
        compiler_params=_cparams(("arbitrary",)),
        name="merge_outproj",
    )(x2, o_a, y, bonus, g, gates, gates, ln_w, ln_b, post_norm, w_a, w_b, w_o, e_mat)


def _ffn_kernel(h_ref, pre_ref, post_ref, wup_ref, cw_ref, cb_ref, wdn_ref, o_ref,
                xn_ref, buf_ref, carry_ref, act_ref):
    tm = h_ref.shape[1]
    ct = FFN_COL_TILE
    n_tiles = D_FF // ct
    halo = SUBLANES

    @pl.when(pl.program_id(1) == 0)
    def _():
        carry_ref[...] = jnp.zeros_like(carry_ref)

    h = h_ref[0]
    ms = jnp.mean(h * h, axis=-1, keepdims=True)
    xn_ref[...] = (h * lax.rsqrt(ms + EPS) * pre_ref[...]).astype(BF16)

    def conv(c0, slot):
        cols = slice(c0, c0 + ct)
        hu = jnp.dot(xn_ref[...], wup_ref[:, cols], preferred_element_type=F32)
        buf_ref[slot, 0:halo, :] = carry_ref[:, cols]
        buf_ref[slot, halo:halo + tm, :] = hu
        carry_ref[:, cols] = hu[tm - halo:tm, :]
        cw = cw_ref[:, cols]
        out = cb_ref[:, cols] + cw[CONV_W - 1:CONV_W, :] * hu
        for j in range(CONV_W - 1):
            back = CONV_W - 1 - j
            out = out + cw[j:j + 1, :] * buf_ref[slot, halo - back:halo - back + tm, :]
        return out

    for i in range(n_tiles):
        gate = conv(i * ct, 0)
        val = conv(D_FF + i * ct, 1)
        act_ref[:, i * ct:(i + 1) * ct] = (_silu(gate) * val).astype(BF16)

    ff = jnp.dot(act_ref[...], wdn_ref[...], preferred_element_type=F32)
    ms2 = jnp.mean(ff * ff, axis=-1, keepdims=True)
    o_ref[0] = h + ff * lax.rsqrt(ms2 + EPS) * post_ref[...]


def _ffn(h1, pre, post, w_up, conv_w, conv_b, w_down, tm):
    bsz, s_len, d = h1.shape
    blk = pl.BlockSpec((1, tm, d), lambda b, i: (b, i, 0))
    row = pl.BlockSpec((1, d), lambda b, i: (0, 0))
    return pl.pallas_call(
        _ffn_kernel,
        out_shape=jax.ShapeDtypeStruct((bsz, s_len, d), F32),
        grid=(bsz, s_len // tm),
        in_specs=[blk, row, row,
                  pl.BlockSpec((d, 2 * D_FF), lambda b, i: (0, 0)),
                  pl.BlockSpec((CONV_W, 2 * D_FF), lambda b, i: (0, 0)),
                  pl.BlockSpec((1, 2 * D_FF), lambda b, i: (0, 0)),
                  pl.BlockSpec((D_FF, d), lambda b, i: (0, 0))],
        out_specs=blk,
        scratch_shapes=[
            pltpu.VMEM((tm, d), BF16),
            pltpu.VMEM((2, tm + SUBLANES, FFN_COL_TILE), F32),
            pltpu.VMEM((SUBLANES, 2 * D_FF), F32),
            pltpu.VMEM((tm, D_FF), BF16),
        ],
        compiler_params=_cparams(("arbitrary", "arbitrary")),
        name="conv_ffn",
    )(h1, pre, post, w_up, conv_w, conv_b, w_down)


def _head_segment_ones():
    lane = np.arange(MXU_WIDTH)
    e_mat = (lane[:, None] // RWKV_HEAD == lane[None, :] // RWKV_HEAD).astype(np.float32)
    return jnp.asarray(e_mat, BF16)


def kernel(x, attn_pre_norm, w_in, hgrn_lb, hgrn_gnorm, w_branch_a, rwkv_mu, rwkv_w0, rwkv_w2,
           rwkv_a0, rwkv_a2, rwkv_g2, rwkv_k_k, rwkv_k_a, rwkv_r_k, rwkv_ln_w, rwkv_ln_b,
           w_branch_b, w_out, attn_post_norm, ffn_pre_norm, w_up, conv_w, conv_b, w_down,
           ffn_post_norm):
    bsz, s_len, d = x.shape
    m = bsz * s_len
    e_mat = _head_segment_ones()
    h = x
    for l in range(w_in.shape[0]):
        x2 = h.reshape(m, d)
        pre = attn_pre_norm[l].reshape(1, d)
        w_l = w_in[l].astype(BF16)
        zq, zf, zi, zg = _inproj(
            x2, pre, w_l[:, :HGRN_COLS], act=None, tm=256, name="inproj_hgrn",
            sections=[(HGRN_FWD, BF16), (HGRN_FWD, F32), (HGRN_IN, BF16), (HGRN_IN, BF16)])
        z_r, = _inproj(x2, pre, w_l[:, HGRN_COLS:HGRN_COLS + RWKV_COLS], act=None, tm=256,
                       name="inproj_rwkv", sections=[(RWKV_COLS, BF16)])
        gates, = _inproj(x2, pre, w_l[:, HGRN_COLS + RWKV_COLS:], act="sigmoid", tm=512,
                         name="inproj_gates", sections=[(GATE_COLS, BF16)])

        o_a = _hgrn(*(t.reshape(bsz, s_len, -1) for t in (zq, zf, zi, zg)), hgrn_lb,
                    hgrn_gnorm[l].reshape(1, d), l, BF16)

        wa2 = jnp.zeros((W_LORA + A_LORA, 2 * d), F32)
        wa2 = wa2.at[:W_LORA, :d].set(rwkv_w2[l]).at[W_LORA:, d:].set(rwkv_a2[l]).astype(BF16)
        y, g, bonus = _rwkv_chunk(
            z_r.reshape(bsz, s_len, RWKV_COLS), rwkv_mu[l].reshape(1, -1),
            rwkv_w0[l].reshape(1, d), rwkv_a0[l].reshape(1, d), rwkv_k_k[l].reshape(1, d),
            rwkv_k_a[l].reshape(1, d), rwkv_r_k[l].reshape(1, d), wa2,
            rwkv_g2[l].astype(BF16), e_mat)

        h1 = _merge(x2, o_a.reshape(m, d), y.reshape(m, d), bonus.reshape(m, d),
                    g.reshape(m, d), gates, rwkv_ln_w[l].reshape(1, d),
                    rwkv_ln_b[l].reshape(1, d), attn_post_norm[l].reshape(1, d),
                    w_branch_a[l].astype(BF16), w_branch_b[l].astype(BF16),
                    w_out[l].astype(BF16), e_mat, tm=256)

        h = _ffn(h1.reshape(bsz, s_len, d), ffn_pre_norm[l].reshape(1, d),
                 ffn_post_norm[l].reshape(1, d), w_up[l].astype(BF16), conv_w[l],
                 conv_b[l].reshape(1, -1), w_down[l].astype(BF16), tm=256)
    return h
```

```python
import functools

import jax
import jax.numpy as jnp
import numpy as np
from jax import lax
from jax.experimental import pallas as pl
from jax.experimental.pallas import tpu as pltpu

F32 = jnp.float32
BF16 = jnp.bfloat16

D_MODEL = 1024
HGRN_HEADS = 8
HGRN_EXPAND = 128
HGRN_FWD = HGRN_HEADS * HGRN_EXPAND
HGRN_IN = D_MODEL
HGRN_SCALE = HGRN_EXPAND ** -0.5
CHUNK = 32
RWKV_HEAD = 64
RWKV_DIM = D_MODEL
W_LORA = 64
A_LORA = 64
G_LORA = 128
GN_EPS = 1e-5 * RWKV_HEAD
D_FF = 2816
CONV_W = 3
EPS = 1e-6
HGRN_COLS = 2 * HGRN_FWD + 2 * HGRN_IN
RWKV_COLS = 3 * RWKV_DIM + W_LORA + A_LORA + G_LORA
GATE_COLS = 2 * D_MODEL

SUBLANES = 8
LANES = 128
MXU_WIDTH = 256
VMEM_LIMIT_BYTES = 56 * 1024 * 1024

HGRN_CHUNK_GROUP = 8
FFN_COL_TILE = 256


def _sigmoid(x):
    return 1.0 / (1.0 + jnp.exp(-x))


def _silu(x):
    return x * _sigmoid(x)


def _bdot(a, b):
    return jnp.dot(a.astype(BF16), b.astype(BF16), preferred_element_type=F32)


def _segment_sum_bcast(x, e_ref):
    n = x.shape[-1]
    e = e_ref[...]
    outs = []
    for j in range(n // MXU_WIDTH):
        outs.append(jnp.dot(x[:, j * MXU_WIDTH:(j + 1) * MXU_WIDTH].astype(BF16), e,
                            preferred_element_type=F32))
    return jnp.concatenate(outs, axis=-1)


def _cparams(semantics):
    return pltpu.CompilerParams(dimension_semantics=semantics,
                                vmem_limit_bytes=VMEM_LIMIT_BYTES)


def _inproj_kernel(x_ref, g_ref, w_ref, *o_refs, act):
    x = x_ref[...]
    ms = jnp.mean(x * x, axis=-1, keepdims=True)
    xn = x * lax.rsqrt(ms + EPS) * g_ref[...]
    z = jnp.dot(xn.astype(BF16), w_ref[...], preferred_element_type=F32)
    if act == "sigmoid":
        z = _sigmoid(z)
    off = 0
    for o_ref in o_refs:
        n = o_ref.shape[1]
        o_ref[...] = z[:, off:off + n].astype(o_ref.dtype)
        off += n


def _inproj(x2, gain, w, *, sections, act, tm, name):
    m, d = x2.shape
    n = w.shape[1]
    assert sum(cols for cols, _ in sections) == n
    return pl.pallas_call(
        functools.partial(_inproj_kernel, act=act),
        out_shape=[jax.ShapeDtypeStruct((m, cols), dt) for cols, dt in sections],
        grid=(m // tm,),
        in_specs=[
            pl.BlockSpec((tm, d), lambda i: (i, 0)),
            pl.BlockSpec((1, d), lambda i: (0, 0)),
            pl.BlockSpec((d, n), lambda i: (0, 0)),
        ],
        out_specs=[pl.BlockSpec((tm, cols), lambda i: (i, 0)) for cols, _ in sections],
        compiler_params=_cparams(("arbitrary",)),
        name=name,
    )(x2, gain, w)


def _hgrn_kernel(hq_ref, hf_ref, hi_ref, hg_ref, lb_ref, gn_ref, o_ref, state_ref, *, layer):
    s_len = hq_ref.shape[1]
    rows = HGRN_CHUNK_GROUP * CHUNK
    n_groups = s_len // rows

    lbp = lb_ref[...]
    lbm = jnp.max(lbp, axis=0, keepdims=True)
    lbe = jnp.exp(lbp - lbm)
    lb = (jnp.sum(lbe[0:layer + 1, :], axis=0, keepdims=True)
          / jnp.sum(lbe, axis=0, keepdims=True))
    gn = gn_ref[...]

    state_ref[...] = jnp.zeros_like(state_ref)

    row_in_chunk = lax.broadcasted_iota(jnp.int32, (rows, HGRN_EXPAND), 0) % CHUNK
    ci = lax.broadcasted_iota(jnp.int32, (CHUNK, CHUNK), 0)
    di = lax.broadcasted_iota(jnp.int32, (CHUNK, CHUNK), 1)
    tril = ci >= di

    def group(gi, carry):
        r0 = pl.multiple_of(gi * rows, rows)
        hq = hq_ref[0, pl.ds(r0, rows), :].astype(F32)
        hf = hf_ref[0, pl.ds(r0, rows), :]
        v = hi_ref[0, pl.ds(r0, rows), :].astype(F32)
        hg = hg_ref[0, pl.ds(r0, rows), :].astype(F32)
        q = _silu(hq) * HGRN_SCALE
        f = lb + (1.0 - lb) * _sigmoid(hf)
        k = 1.0 - f
        b = jnp.log(f)
        sh = 1
        while sh < CHUNK:
            b = b + jnp.where(row_in_chunk >= sh, pltpu.roll(b, sh, axis=0), 0.0)
            sh *= 2
        chunks = []
        for j in range(HGRN_CHUNK_GROUP):
            sl = slice(j * CHUNK, (j + 1) * CHUNK)
            bj, qj, kj, vj = b[sl], q[sl], k[sl], v[sl]
            b_mid = bj[CHUNK // 2 - 1:CHUNK // 2, :]
            b_last = bj[CHUNK - 1:CHUNK, :]
            q_in = qj * jnp.exp(bj - b_mid)
            k_in = kj * jnp.exp(b_mid - bj)
            scores = lax.dot_general(q_in.astype(BF16), k_in.astype(BF16),
                                     (((1,), (1,)), ((), ())), preferred_element_type=F32)
            k_dec = k_in * jnp.exp(b_last - b_mid)
            u_t = _bdot(vj.T, k_dec)
            chunks.append((scores, u_t, q_in * jnp.exp(b_mid), jnp.exp(b_last), vj))
        o_intra = [_bdot(jnp.where(tril, scores, 0.0), vj) for scores, _, _, _, vj in chunks]
        outs = []
        st = state_ref[...]
        for j, (_, u_t, q_dec, decay, _) in enumerate(chunks):
            o_inter = lax.dot_general(q_dec.astype(BF16), st.astype(BF16),
                                      (((1,), (1,)), ((), ())), preferred_element_type=F32)
            st = st * decay + u_t
            outs.append(o_intra[j] + o_inter)
        state_ref[...] = st
        o = jnp.concatenate(outs, axis=0)
        o = o * lax.rsqrt(jnp.mean(o * o, axis=-1, keepdims=True) + EPS)
        o = o * gn * _silu(hg)
        o_ref[0, pl.ds(r0, rows), :] = o.astype(o_ref.dtype)
        return carry

    lax.fori_loop(0, n_groups, group, 0, unroll=4)


def _hgrn(zq, zf, zi, zg, hgrn_lb, gnorm, layer, out_dtype):
    bsz, s_len, _ = zq.shape
    col = pl.BlockSpec((1, s_len, HGRN_EXPAND), lambda b, h: (b, 0, h))

    return pl.pallas_call(
        functools.partial(_hgrn_kernel, layer=layer),
        out_shape=jax.ShapeDtypeStruct((bsz, s_len, HGRN_IN), out_dtype),
        grid=(bsz, HGRN_HEADS),
        in_specs=[
            col, col, col, col,
            pl.BlockSpec((hgrn_lb.shape[0], HGRN_EXPAND), lambda b, h: (0, h)),
            pl.BlockSpec((1, HGRN_EXPAND), lambda b, h: (0, h)),
        ],
        out_specs=pl.BlockSpec((1, s_len, HGRN_EXPAND), lambda b, h: (b, 0, h)),
        scratch_shapes=[pltpu.VMEM((HGRN_EXPAND, HGRN_EXPAND), F32)],
        compiler_params=_cparams(("arbitrary", "arbitrary")),
        name="hgrn2_chunkwise",
    )(zq, zf, zi, zg, hgrn_lb, gnorm)


RWKV_CHUNK = 64
HEADS_PER_TILE = MXU_WIDTH // RWKV_HEAD
RWKV_GROUP_BATCHES = 2
RWKV_PROLOGUE_LEAD = 2


def _rwkv_prep_shared(z_lora, zs_lora, mu_ref):
    n = W_LORA + A_LORA
    zm = z_lora + mu_ref[:, 3 * RWKV_DIM:] * (zs_lora - z_lora)
    wa = zm[:, 0:n]
    lane = lax.broadcasted_iota(jnp.int32, wa.shape, 1)
    wa = jnp.where(lane < W_LORA, jnp.tanh(wa), wa)
    return wa.astype(BF16), _sigmoid(zm[:, n:]).astype(BF16)


def _rwkv_prep_tile(cols, z3, zs3, wa, gs, mu_ref, w0_ref, a0_ref, kkw_ref, ka_ref, rk_ref,
                    wa2_ref, g2_ref, e_ref):
    d = RWKV_DIM
    rr, kr, vr = (z + mu_ref[:, sec * d + cols.start:sec * d + cols.stop] * (zs - z)
                  for sec, (z, zs) in enumerate(zip(z3, zs3)))
    e = e_ref[...]
    seg = lambda x: jnp.dot(x.astype(BF16), e, preferred_element_type=F32)
    w_pre = w0_ref[:, cols] + jnp.dot(wa, wa2_ref[:, cols], preferred_element_type=F32)
    a_pre = a0_ref[:, cols] + jnp.dot(wa, wa2_ref[:, d + cols.start:d + cols.stop],
                                      preferred_element_type=F32)
    log_decay = -float(np.exp(-0.5)) * _sigmoid(w_pre)
    a = _sigmoid(a_pre)
    g = jnp.dot(gs, g2_ref[:, cols], preferred_element_type=F32)
    kk = kr * kkw_ref[:, cols]
    kk = kk / jnp.maximum(jnp.sqrt(seg(kk * kk)), 1e-12)
    k2 = kr * (1.0 + (a - 1.0) * ka_ref[:, cols])
    bonus = seg(rr * k2 * rk_ref[:, cols]) * vr
    return rr, log_decay, k2, vr, kk, kk * a, g, bonus


def _rwkv_chunk_kernel(z_ref, mu_ref, w0_ref, a0_ref, kkw_ref, ka_ref, rk_ref, wa2_ref, g2_ref,
                       e_ref, mask_ref, tri_ref, y_ref, g_ref, bonus_ref, s_ref, carry_ref):
    bsz = z_ref.shape[0]
    c = RWKV_CHUNK
    n_tiles = RWKV_DIM // MXU_WIDTH

    @pl.when(pl.program_id(0) == 0)
    def _():
        s_ref[...] = jnp.zeros_like(s_ref)
        carry_ref[...] = jnp.zeros_like(carry_ref)

    lane = lax.broadcasted_iota(jnp.int32, (c, MXU_WIDTH), 1)
    head_masks = [lane // RWKV_HEAD == h for h in range(HEADS_PER_TILE)]

    def mask(name):
        return mask_ref[RWKV_MASKS.index(name)]

    def bd(x):
        xb = x.astype(BF16)
        zero = jnp.zeros_like(xb)
        return jnp.concatenate([jnp.where(m, xb, zero) for m in head_masks], axis=0)

    def pk(a, b_bd):
        return jnp.dot(a.astype(BF16), b_bd, preferred_element_type=F32)

    def pk_t(a, b_bd):
        return lax.dot_general(a.astype(BF16), b_bd, (((1,), (1,)), ((), ())),
                               preferred_element_type=F32)

    def tile(r, lw, k, v, kk, kb, s0):
        lw_hi = lw.astype(BF16)
        lw_lo = (lw - lw_hi.astype(F32)).astype(BF16)
        tri = tri_ref[...]
        g = (jnp.dot(tri, lw_hi, preferred_element_type=F32)
             + jnp.dot(tri, lw_lo, preferred_element_type=F32))
        g_mid = g[c // 2 - 1:c // 2, :]
        g_last = g[c - 1:c, :]
        e_neg = jnp.exp(g_mid - g)
        a_t = -kk * jnp.exp(g - lw - g_mid)
        r_t = r * jnp.exp(g - g_mid)
        b_t = kb * e_neg
        k_t = k * e_neg
        e_last = jnp.exp(g_last - g_mid)
        for _ in range(RWKV_PROLOGUE_LEAD):
            yield

        ar = jnp.concatenate([a_t, r_t], axis=0)
        nb = pk_t(ar, bd(b_t))
        nk = pk_t(ar, bd(k_t))
        yield
        n_ab = nb[0:c] * mask("strict")
        m_rb = nb[c:] * mask("incl")
        n_ak = nk[0:c] * mask("strict")
        m_rk = nk[c:] * mask("incl")

        n8 = n_ab * mask("block8")
        t_inv = mask("eye") + n8
        n_pow = pk(n8, bd(n8))
        qy = pk(jnp.concatenate([n_ak, m_rk], axis=0), bd(v))
        yield
        t_inv = t_inv + pk(n_pow, bd(t_inv))
        n_pow = pk(n_pow, bd(n_pow))
        yield
        t_inv = t_inv + pk(n_pow, bd(t_inv))
        yield
        for size in (16, 32, 64):
            z = pk(t_inv, bd(n_ab * mask("off%d" % size)))
            yield
            t_inv = t_inv + pk(z, bd(t_inv))
            yield

        w_mat = pk(t_inv, bd(a_t))
        u0 = pk(t_inv, bd(qy[0:c]))
        yield
        x = pk_t(jnp.concatenate([w_mat, r_t], axis=0), bd(s0 * jnp.exp(g_mid)))
        yield
        p = x[0:c] + u0
        y = x[c:] + qy[c:] + pk(m_rb, bd(p))
        upd = lax.dot_general(jnp.concatenate([p, v], axis=0).astype(BF16),
                              jnp.concatenate([b_t * e_last, k_t * e_last], axis=0).astype(BF16),
                              (((0,), (0,)), ((), ())), preferred_element_type=F32)
        yield
        s_new = s0 * jnp.exp(g_last)
        for h in range(HEADS_PER_TILE):
            s_new = s_new + upd[h * RWKV_HEAD:(h + 1) * RWKV_HEAD] * mask("head%d" % h)
        return y, s_new

    d = RWKV_DIM
    first_row = lax.broadcasted_iota(jnp.int32, (c, MXU_WIDTH), 0) == 0

    def load_rows(batches, cols):
        z = [z_ref[b, :, cols].astype(F32) for b in batches]
        zs = [jnp.where(first_row[:, 0:cols.stop - cols.start], carry_ref[b, 0:1, cols],
                        pltpu.roll(zb, 1, axis=0)) for b, zb in zip(batches, z)]
        return jnp.concatenate(z, axis=0), jnp.concatenate(zs, axis=0)

    def start_unit(batches, j, shared):
        cols = slice(j * MXU_WIDTH, (j + 1) * MXU_WIDTH)
        sections = [load_rows(batches, slice(sec * d + cols.start, sec * d + cols.stop))
                    for sec in range(3)]
        prep = _rwkv_prep_tile(cols, [zz for zz, _ in sections], [zs for _, zs in sections], *shared,
                               mu_ref, w0_ref, a0_ref, kkw_ref, ka_ref, rk_ref, wa2_ref, g2_ref,
                               e_ref)
        r, lw, k, v, kk, kb, g, bonus = prep
        gens = [tile(*(t[bb * c:(bb + 1) * c] for t in (r, lw, k, v, kk, kb)), s_ref[b, :, cols])
                for bb, b in enumerate(batches)]

        def finish(results):
            for bb, b in enumerate(batches):
                y, s_new = results[bb]
                rows = slice(bb * c, (bb + 1) * c)
                g_ref[b, :, cols] = g[rows].astype(g_ref.dtype)
                bonus_ref[b, :, cols] = bonus[rows].astype(bonus_ref.dtype)
                y_ref[b, :, cols] = y.astype(y_ref.dtype)
                s_ref[b, :, cols] = s_new

        return gens, finish

    units = []
    for g0 in range(0, bsz, RWKV_GROUP_BATCHES):
        batches = list(range(g0, g0 + RWKV_GROUP_BATCHES))
        for j in range(n_tiles):
            units.append((batches, j))
    running = []
    shared = None
    while units or running:
        if units:
            batches, j = units.pop(0)
            if j == 0:
                shared = _rwkv_prep_shared(*load_rows(batches, slice(3 * d, RWKV_COLS)), mu_ref)
            gens, finish = start_unit(batches, j, shared)
            running.append([gens, [None] * len(gens), finish])
        for entry in list(running):
            gens, results, finish = entry
            for idx, gen in enumerate(gens):
                if results[idx] is None:
                    try:
                        next(gen)
                    except StopIteration as stop:
                        results[idx] = stop.value
            if all(res is not None for res in results):
                finish(results)
                running.remove(entry)
    carry_ref[:, 0:1, :] = z_ref[:, c - 1:c, :].astype(F32)


RWKV_MASKS = ("strict", "incl", "eye", "block8", "off16", "off32", "off64",
              "head0", "head1", "head2", "head3")


def _rwkv_chunk_constants():
    c = RWKV_CHUNK
    t = np.arange(c)[:, None]
    lane = np.arange(MXU_WIDTH)[None, :]
    s = lane % c
    same = lambda size: (t // size) == (s // size)
    table = {"strict": s < t, "incl": s <= t, "eye": s == t, "block8": (s < t) & same(8)}
    for size in (16, 32, 64):
        table["off%d" % size] = (s < t) & same(size) & ~same(size // 2)
    for h in range(HEADS_PER_TILE):
        table["head%d" % h] = np.broadcast_to(lane // RWKV_HEAD == h, (c, MXU_WIDTH))
    masks = np.stack([np.broadcast_to(table[name], (c, MXU_WIDTH)) for name in RWKV_MASKS])
    tri = np.arange(c)[:, None] >= np.arange(c)[None, :]
    return jnp.asarray(masks.astype(np.float32)), jnp.asarray(tri.astype(np.float32), BF16)


def _rwkv_chunk(z_r, mu, w0, a0, k_k, k_a, r_k, wa2, g2, e_mat):
    bsz, s_len, _ = z_r.shape
    d = RWKV_DIM
    assert bsz % RWKV_GROUP_BATCHES == 0 and s_len % RWKV_CHUNK == 0
    whole = lambda a: pl.BlockSpec(a.shape, lambda i: (0,) * a.ndim)
    out_blk = pl.BlockSpec((bsz, RWKV_CHUNK, d), lambda i: (0, i, 0))
    params = (mu, w0, a0, k_k, k_a, r_k, wa2, g2, e_mat) + _rwkv_chunk_constants()
    return pl.pallas_call(
        _rwkv_chunk_kernel,
        out_shape=[jax.ShapeDtypeStruct((bsz, s_len, d), BF16)] * 3,
        grid=(s_len // RWKV_CHUNK,),
        in_specs=[pl.BlockSpec((bsz, RWKV_CHUNK, RWKV_COLS), lambda i: (0, i, 0))]
        + [whole(p) for p in params],
        out_specs=[out_blk] * 3,
        scratch_shapes=[pltpu.VMEM((bsz, RWKV_HEAD, d), F32),
                        pltpu.VMEM((bsz, SUBLANES, RWKV_COLS), F32)],
        compiler_params=_cparams(("arbitrary",)),
        name="rwkv7_chunk",
    )(z_r, *params)


def _merge_kernel(x_ref, oa_ref, y_ref, bonus_ref, g_ref, ga_ref, gb_ref, lnw_ref, lnb_ref,
                  pn_ref, wa_ref, wb_ref, wo_ref, e_ref, o_ref):
    y = y_ref[...].astype(F32)
    inv_n = 1.0 / RWKV_HEAD
    mu = _segment_sum_bcast(y, e_ref) * inv_n
    yc = y - mu
    var = _segment_sum_bcast(yc * yc, e_ref) * inv_n
    yn = yc * lax.rsqrt(var + GN_EPS) * lnw_ref[...] + lnb_ref[...]
    o_b = (yn + bonus_ref[...].astype(F32)) * g_ref[...].astype(F32)
    y_b = jnp.dot(o_b.astype(BF16), wb_ref[...], preferred_element_type=F32)
    y_a = jnp.dot(oa_ref[...].astype(BF16), wa_ref[...], preferred_element_type=F32)
    merged = ga_ref[...].astype(F32) * y_a + gb_ref[...].astype(F32) * y_b
    mix = jnp.dot(merged.astype(BF16), wo_ref[...], preferred_element_type=F32)
    ms = jnp.mean(mix * mix, axis=-1, keepdims=True)
    o_ref[...] = x_ref[...] + mix * lax.rsqrt(ms + EPS) * pn_ref[...]


def _merge(x2, o_a, y, bonus, g, gates, ln_w, ln_b, post_norm, w_a, w_b, w_o, e_mat, tm):
    m, d = x2.shape
    blk = pl.BlockSpec((tm, d), lambda i: (i, 0))
    row = pl.BlockSpec((1, d), lambda i: (0, 0))
    wsp = pl.BlockSpec((d, d), lambda i: (0, 0))
    return pl.pallas_call(
        _merge_kernel,
        out_shape=jax.ShapeDtypeStruct((m, d), F32),
        grid=(m // tm,),
        in_specs=[blk, blk, blk, blk, blk,
                  pl.BlockSpec((tm, d), lambda i: (i, 0)),
                  pl.BlockSpec((tm, d), lambda i: (i, 1)),
                  row, row, row, wsp, wsp, wsp,
                  pl.BlockSpec((MXU_WIDTH, MXU_WIDTH), lambda i: (0, 0))],
        out_specs=blk,
        compiler_params=_cparams(("arbitrary",)),
        name="merge_outproj",
    )(x2, o_a, y, bonus, g, gates, gates, ln_w, ln_b, post_norm, w_a, w_b, w_o, e_mat)


def _ffn_kernel(h_ref, pre_ref, post_ref, wup_ref, cw_ref, cb_ref, wdn_ref, o_ref,
                xn_ref, buf_ref, carry_ref, act_ref):
    tm = h_ref.shape[1]
    ct = FFN_COL_TILE
    n_tiles = D_FF // ct
    halo = SUBLANES

    @pl.when(pl.program_id(1) == 0)
    def _():
        carry_ref[...] = jnp.zeros_like(carry_ref)

    h = h_ref[0]
    ms = jnp.mean(h * h, axis=-1, keepdims=True)
    xn_ref[...] = (h * lax.rsqrt(ms + EPS) * pre_ref[...]).astype(BF16)

    def conv(c0, slot):
        cols = slice(c0, c0 + ct)
        hu = jnp.dot(xn_ref[...], wup_ref[:, cols], preferred_element_type=F32)
        buf_ref[slot, 0:halo, :] = carry_ref[:, cols]
        buf_ref[slot, halo:halo + tm, :] = hu
        carry_ref[:, cols] = hu[tm - halo:tm, :]
        cw = cw_ref[:, cols]
        out = cb_ref[:, cols] + cw[CONV_W - 1:CONV_W, :] * hu
        for j in range(CONV_W - 1):
            back = CONV_W - 1 - j
            out = out + cw[j:j + 1, :] * buf_ref[slot, halo - back:halo - back + tm, :]
        return out

    for i in range(n_tiles):
        gate = conv(i * ct, 0)
        val = conv(D_FF + i * ct, 1)
        act_ref[:, i * ct:(i + 1) * ct] = (_silu(gate) * val).astype(BF16)

    ff = jnp.dot(act_ref[...], wdn_ref[...], preferred_element_type=F32)
    ms2 = jnp.mean(ff * ff, axis=-1, keepdims=True)
    o_ref[0] = h + ff * lax.rsqrt(ms2 + EPS) * post_ref[...]


def _ffn(h1, pre, post, w_up, conv_w, conv_b, w_down, tm):
    bsz, s_len, d = h1.shape
    blk = pl.BlockSpec((1, tm, d), lambda b, i: (b, i, 0))
    row = pl.BlockSpec((1, d), lambda b, i: (0, 0))
    return pl.pallas_call(
        _ffn_kernel,
        out_shape=jax.ShapeDtypeStruct((bsz, s_len, d), F32),
        grid=(bsz, s_len // tm),
        in_specs=[blk, row, row,
                  pl.BlockSpec((d, 2 * D_FF), lambda b, i: (0, 0)),
                  pl.BlockSpec((CONV_W, 2 * D_FF), lambda b, i: (0, 0)),
                  pl.BlockSpec((1, 2 * D_FF), lambda b, i: (0, 0)),
                  pl.BlockSpec((D_FF, d), lambda b, i: (0, 0))],
        out_specs=blk,
        scratch_shapes=[
            pltpu.VMEM((tm, d), BF16),
            pltpu.VMEM((2, tm + SUBLANES, FFN_COL_TILE), F32),
            pltpu.VMEM((SUBLANES, 2 * D_FF), F32),
            pltpu.VMEM((tm, D_FF), BF16),
        ],
        compiler_params=_cparams(("arbitrary", "arbitrary")),
        name="conv_ffn",
    )(h1, pre, post, w_up, conv_w, conv_b, w_down)


def _head_segment_ones():
    lane = np.arange(MXU_WIDTH)
    e_mat = (lane[:, None] // RWKV_HEAD == lane[None, :] // RWKV_HEAD).astype(np.float32)
    return jnp.asarray(e_mat, BF16)


def kernel(x, attn_pre_norm, w_in, hgrn_lb, hgrn_gnorm, w_branch_a, rwkv_mu, rwkv_w0, rwkv_w2,
           rwkv_a0, rwkv_a2, rwkv_g2, rwkv_k_k, rwkv_k_a, rwkv_r_k, rwkv_ln_w, rwkv_ln_b,
           w_branch_b, w_out, attn_post_norm, ffn_pre_norm, w_up, conv_w, conv_b, w_down,
           ffn_post_norm):
    bsz, s_len, d = x.shape
    m = bsz * s_len
    e_mat = _head_segment_ones()
    h = x
    for l in range(w_in.shape[0]):
        x2 = h.reshape(m, d)
        pre = attn_pre_norm[l].reshape(1, d)
        w_l = w_in[l].astype(BF16)
        zq, zf, zi, zg = _inproj(
            x2, pre, w_l[:, :HGRN_COLS], act=None, tm=256, name="inproj_hgrn",
            sections=[(HGRN_FWD, BF16), (HGRN_FWD, F32), (HGRN_IN, BF16), (HGRN_IN, BF16)])
        z_r, = _inproj(x2, pre, w_l[:, HGRN_COLS:HGRN_COLS + RWKV_COLS], act=None, tm=256,
                       name="inproj_rwkv", sections=[(RWKV_COLS, BF16)])
        gates, = _inproj(x2, pre, w_l[:, HGRN_COLS + RWKV_COLS:], act="sigmoid", tm=512,
                         name="inproj_gates", sections=[(GATE_COLS, BF16)])

        o_a = _hgrn(*(t.reshape(bsz, s_len, -1) for t in (zq, zf, zi, zg)), hgrn_lb,
                    hgrn_gnorm[l].reshape(1, d), l, BF16)

        wa2 = jnp.zeros((W_LORA + A_LORA, 2 * d), F32)
        wa2 = wa2.at[:W_LORA, :d].set(rwkv_w2[l]).at[W_LORA:, d:].set(rwkv_a2[l]).astype(BF16)
        y, g, bonus = _rwkv_chunk(
            z_r.reshape(bsz, s_len, RWKV_COLS), rwkv_mu[l].reshape(1, -1),
            rwkv_w0[l].reshape(1, d), rwkv_a0[l].reshape(1, d), rwkv_k_k[l].reshape(1, d),
            rwkv_k_a[l].reshape(1, d), rwkv_r_k[l].reshape(1, d), wa2,
            rwkv_g2[l].astype(BF16), e_mat)

        h1 = _merge(x2, o_a.reshape(m, d), y.reshape(m, d), bonus.reshape(m, d),
                    g.reshape(m, d), gates, rwkv_ln_w[l].reshape(1, d),
                    rwkv_ln_b[l].reshape(1, d), attn_post_norm[l].reshape(1, d),
                    w_branch_a[l].astype(BF16), w_branch_b[l].astype(BF16),
                    w_out[l].astype(BF16), e_mat, tm=512)

        h = _ffn(h1.reshape(bsz, s_len, d), ffn_pre_norm[l].reshape(1, d),
                 ffn_post_norm[l].reshape(1, d), w_up[l].astype(BF16), conv_w[l],
                 conv_b[l].reshape(1, -1), w_down[l].astype(BF16), tm=512)
    return h
```

```python
import functools

import jax
import jax.numpy as jnp
import numpy as np
from jax import lax
from jax.experimental import pallas as pl
from jax.experimental.pallas import tpu as pltpu

F32 = jnp.float32
BF16 = jnp.bfloat16

D_MODEL = 1024
HGRN_HEADS = 8
HGRN_EXPAND = 128
HGRN_FWD = HGRN_HEADS * HGRN_EXPAND
HGRN_IN = D_MODEL
HGRN_SCALE = HGRN_EXPAND ** -0.5
CHUNK = 32
RWKV_HEAD = 64
RWKV_DIM = D_MODEL
W_LORA = 64
A_LORA = 64
G_LORA = 128
GN_EPS = 1e-5 * RWKV_HEAD
D_FF = 2816
CONV_W = 3
EPS = 1e-6
HGRN_COLS = 2 * HGRN_FWD + 2 * HGRN_IN
RWKV_COLS = 3 * RWKV_DIM + W_LORA + A_LORA + G_LORA
GATE_COLS = 2 * D_MODEL

SUBLANES = 8
LANES = 128
MXU_WIDTH = 256
VMEM_LIMIT_BYTES = 56 * 1024 * 1024

HGRN_CHUNK_GROUP = 8
FFN_COL_TILE = 256


def _sigmoid(x):
    return 1.0 / (1.0 + jnp.exp(-x))


def _silu(x):
    return x * _sigmoid(x)


def _bdot(a, b):
    return jnp.dot(a.astype(BF16), b.astype(BF16), preferred_element_type=F32)


def _segment_sum_bcast(x, e_ref):
    n = x.shape[-1]
    e = e_ref[...]
    outs = []
    for j in range(n // MXU_WIDTH):
        outs.append(jnp.dot(x[:, j * MXU_WIDTH:(j + 1) * MXU_WIDTH].astype(BF16), e,
                            preferred_element_type=F32))
    return jnp.concatenate(outs, axis=-1)


def _cparams(semantics):
    return pltpu.CompilerParams(dimension_semantics=semantics,
                                vmem_limit_bytes=VMEM_LIMIT_BYTES)


def _inproj_kernel(x_ref, g_ref, w_ref, *o_refs, acts):
    x = x_ref[...]
    ms = jnp.mean(x * x, axis=-1, keepdims=True)
    xn = (x * lax.rsqrt(ms + EPS) * g_ref[...]).astype(BF16)
    off = 0
    for o_ref, act in zip(o_refs, acts):
        n = o_ref.shape[1]
        z = jnp.dot(xn, w_ref[:, off:off + n], preferred_element_type=F32)
        if act == "sigmoid":
            z = _sigmoid(z)
        o_ref[...] = z.astype(o_ref.dtype)
        off += n


def _inproj(x2, gain, w, *, sections, tm):
    m, d = x2.shape
    n = w.shape[1]
    assert sum(cols for cols, _, _ in sections) == n
    return pl.pallas_call(
        functools.partial(_inproj_kernel, acts=[act for _, _, act in sections]),
        out_shape=[jax.ShapeDtypeStruct((m, cols), dt) for cols, dt, _ in sections],
        grid=(m // tm,),
        in_specs=[
            pl.BlockSpec((tm, d), lambda i: (i, 0)),
            pl.BlockSpec((1, d), lambda i: (0, 0)),
            pl.BlockSpec((d, n), lambda i: (0, 0), pipeline_mode=pl.Buffered(1)),
        ],
        out_specs=[pl.BlockSpec((tm, cols), lambda i: (i, 0)) for cols, _, _ in sections],
        compiler_params=_cparams(("arbitrary",)),
        name="inproj",
    )(x2, gain, w)


def _hgrn_kernel(hq_ref, hf_ref, hi_ref, hg_ref, lb_ref, gn_ref, o_ref, state_ref, *, layer):
    s_len = hq_ref.shape[1]
    rows = HGRN_CHUNK_GROUP * CHUNK
    n_groups = s_len // rows

    lbp = lb_ref[...]
    lbm = jnp.max(lbp, axis=0, keepdims=True)
    lbe = jnp.exp(lbp - lbm)
    lb = (jnp.sum(lbe[0:layer + 1, :], axis=0, keepdims=True)
          / jnp.sum(lbe, axis=0, keepdims=True))
    gn = gn_ref[...]

    state_ref[...] = jnp.zeros_like(state_ref)

    row_in_chunk = lax.broadcasted_iota(jnp.int32, (rows, HGRN_EXPAND), 0) % CHUNK
    ci = lax.broadcasted_iota(jnp.int32, (CHUNK, CHUNK), 0)
    di = lax.broadcasted_iota(jnp.int32, (CHUNK, CHUNK), 1)
    tril = ci >= di

    def group(gi, carry):
        r0 = pl.multiple_of(gi * rows, rows)
        hq = hq_ref[0, pl.ds(r0, rows), :].astype(F32)
        hf = hf_ref[0, pl.ds(r0, rows), :]
        v = hi_ref[0, pl.ds(r0, rows), :].astype(F32)
        hg = hg_ref[0, pl.ds(r0, rows), :].astype(F32)
        q = _silu(hq) * HGRN_SCALE
        f = lb + (1.0 - lb) * _sigmoid(hf)
        k = 1.0 - f
        b = jnp.log(f)
        sh = 1
        while sh < CHUNK:
            b = b + jnp.where(row_in_chunk >= sh, pltpu.roll(b, sh, axis=0), 0.0)
            sh *= 2
        chunks = []
        for j in range(HGRN_CHUNK_GROUP):
            sl = slice(j * CHUNK, (j + 1) * CHUNK)
            bj, qj, kj, vj = b[sl], q[sl], k[sl], v[sl]
            b_mid = bj[CHUNK // 2 - 1:CHUNK // 2, :]
            b_last = bj[CHUNK - 1:CHUNK, :]
            q_in = qj * jnp.exp(bj - b_mid)
            k_in = kj * jnp.exp(b_mid - bj)
            scores = lax.dot_general(q_in.astype(BF16), k_in.astype(BF16),
                                     (((1,), (1,)), ((), ())), preferred_element_type=F32)
            k_dec = k_in * jnp.exp(b_last - b_mid)
            u_t = _bdot(vj.T, k_dec)
            chunks.append((scores, u_t, q_in * jnp.exp(b_mid), jnp.exp(b_last), vj))
        o_intra = [_bdot(jnp.where(tril, scores, 0.0), vj) for scores, _, _, _, vj in chunks]
        outs = []
        st = state_ref[...]
        for j, (_, u_t, q_dec, decay, _) in enumerate(chunks):
            o_inter = lax.dot_general(q_dec.astype(BF16), st.astype(BF16),
                                      (((1,), (1,)), ((), ())), preferred_element_type=F32)
            st = st * decay + u_t
            outs.append(o_intra[j] + o_inter)
        state_ref[...] = st
        o = jnp.concatenate(outs, axis=0)
        o = o * lax.rsqrt(jnp.mean(o * o, axis=-1, keepdims=True) + EPS)
        o = o * gn * _silu(hg)
        o_ref[0, pl.ds(r0, rows), :] = o.astype(o_ref.dtype)
        return carry

    lax.fori_loop(0, n_groups, group, 0, unroll=8)


def _hgrn(zq, zf, zi, zg, hgrn_lb, gnorm, layer, out_dtype):
    bsz, s_len, _ = zq.shape
    col = pl.BlockSpec((1, s_len, HGRN_EXPAND), lambda b, h: (b, 0, h))

    return pl.pallas_call(
        functools.partial(_hgrn_kernel, layer=layer),
        out_shape=jax.ShapeDtypeStruct((bsz, s_len, HGRN_IN), out_dtype),
        grid=(bsz, HGRN_HEADS),
        in_specs=[
            col, col, col, col,
            pl.BlockSpec((hgrn_lb.shape[0], HGRN_EXPAND), lambda b, h: (0, h)),
            pl.BlockSpec((1, HGRN_EXPAND), lambda b, h: (0, h)),
        ],
        out_specs=pl.BlockSpec((1, s_len, HGRN_EXPAND), lambda b, h: (b, 0, h)),
        scratch_shapes=[pltpu.VMEM((HGRN_EXPAND, HGRN_EXPAND), F32)],
        compiler_params=_cparams(("arbitrary", "arbitrary")),
        name="hgrn2_chunkwise",
    )(zq, zf, zi, zg, hgrn_lb, gnorm)


RWKV_CHUNK = 64
HEADS_PER_TILE = MXU_WIDTH // RWKV_HEAD
RWKV_GROUP_BATCHES = 2
RWKV_UNITS_PER_STAGE = 1
RWKV_PROLOGUE_LEAD = 2


def _rwkv_prep_shared(z_lora, zs_lora, mu_ref):
    n = W_LORA + A_LORA
    zm = z_lora + mu_ref[:, 3 * RWKV_DIM:] * (zs_lora - z_lora)
    wa = zm[:, 0:n]
    lane = lax.broadcasted_iota(jnp.int32, wa.shape, 1)
    wa = jnp.where(lane < W_LORA, jnp.tanh(wa), wa)
    return wa.astype(BF16), _sigmoid(zm[:, n:]).astype(BF16)


def _rwkv_prep_tile(cols, z3, zs3, wa, gs, mu_ref, w0_ref, a0_ref, kkw_ref, ka_ref, rk_ref,
                    wa2_ref, g2_ref, e_ref):
    d = RWKV_DIM
    rr, kr, vr = (z + mu_ref[:, sec * d + cols.start:sec * d + cols.stop] * (zs - z)
                  for sec, (z, zs) in enumerate(zip(z3, zs3)))
    e = e_ref[...]
    seg = lambda x: jnp.dot(x.astype(BF16), e, preferred_element_type=F32)
    w_pre = w0_ref[:, cols] + jnp.dot(wa, wa2_ref[:, cols], preferred_element_type=F32)
    a_pre = a0_ref[:, cols] + jnp.dot(wa, wa2_ref[:, d + cols.start:d + cols.stop],
                                      preferred_element_type=F32)
    log_decay = -float(np.exp(-0.5)) * _sigmoid(w_pre)
    a = _sigmoid(a_pre)
    g = jnp.dot(gs, g2_ref[:, cols], preferred_element_type=F32)
    kk = kr * kkw_ref[:, cols]
    kk = kk / jnp.maximum(jnp.sqrt(seg(kk * kk)), 1e-12)
    k2 = kr * (1.0 + (a - 1.0) * ka_ref[:, cols])
    bonus = seg(rr * k2 * rk_ref[:, cols]) * vr
    return rr, log_decay, k2, vr, kk, kk * a, g, bonus


def _rwkv_chunk_kernel(z_ref, mu_ref, w0_ref, a0_ref, kkw_ref, ka_ref, rk_ref, wa2_ref, g2_ref,
                       e_ref, mask_ref, tri_ref, y_ref, g_ref, bonus_ref, s_ref, carry_ref):
    bsz = z_ref.shape[0]
    c = RWKV_CHUNK
    n_tiles = RWKV_DIM // MXU_WIDTH

    @pl.when(pl.program_id(0) == 0)
    def _():
        s_ref[...] = jnp.zeros_like(s_ref)
        carry_ref[...] = jnp.zeros_like(carry_ref)

    lane = lax.broadcasted_iota(jnp.int32, (c, MXU_WIDTH), 1)
    head_masks = [lane // RWKV_HEAD == h for h in range(HEADS_PER_TILE)]

    def mask(name):
        return mask_ref[RWKV_MASKS.index(name)]

    def bd(x):
        xb = x.astype(BF16)
        zero = jnp.zeros_like(xb)
        return jnp.concatenate([jnp.where(m, xb, zero) for m in head_masks], axis=0)

    def pk(a, b_bd):
        return jnp.dot(a.astype(BF16), b_bd, preferred_element_type=F32)

    def pk_t(a, b_bd):
        return lax.dot_general(a.astype(BF16), b_bd, (((1,), (1,)), ((), ())),
                               preferred_element_type=F32)

    def tile(r, lw, k, v, kk, kb, s0):
        lw_hi = lw.astype(BF16)
        lw_lo = (lw - lw_hi.astype(F32)).astype(BF16)
        tri = tri_ref[...]
        g = (jnp.dot(tri, lw_hi, preferred_element_type=F32)
             + jnp.dot(tri, lw_lo, preferred_element_type=F32))
        g_mid = g[c // 2 - 1:c // 2, :]
        g_last = g[c - 1:c, :]
        e_neg = jnp.exp(g_mid - g)
        a_t = -kk * jnp.exp(g - lw - g_mid)
        r_t = r * jnp.exp(g - g_mid)
        b_t = kb * e_neg
        k_t = k * e_neg
        e_last = jnp.exp(g_last - g_mid)
        for _ in range(RWKV_PROLOGUE_LEAD):
            yield

        ar = jnp.concatenate([a_t, r_t], axis=0)
        nb = pk_t(ar, bd(b_t))
        nk = pk_t(ar, bd(k_t))
        x = pk_t(ar, bd(s0 * jnp.exp(g_mid)))
        yield
        n_ab = nb[0:c] * mask("strict")
        m_rb = nb[c:] * mask("incl")
        n_ak = nk[0:c] * mask("strict")
        m_rk = nk[c:] * mask("incl")

        n8 = n_ab * mask("block8")
        t_inv = mask("eye") + n8
        n_pow = pk(n8, bd(n8))
        qy = pk(jnp.concatenate([n_ak, m_rk], axis=0), bd(v))
        yield
        t_inv = t_inv + pk(n_pow, bd(t_inv))
        n_pow = pk(n_pow, bd(n_pow))
        yield
        t_inv = t_inv + pk(n_pow, bd(t_inv))
        yield
        for size in (16, 32, 64):
            z = pk(t_inv, bd(n_ab * mask("off%d" % size)))
            yield
            t_inv = t_inv + pk(z, bd(t_inv))
            yield

        p = pk(t_inv, bd(x[0:c] + qy[0:c]))
        yield
        y = x[c:] + qy[c:] + pk(m_rb, bd(p))
        upd = lax.dot_general(jnp.concatenate([p, v], axis=0).astype(BF16),
                              jnp.concatenate([b_t * e_last, k_t * e_last], axis=0).astype(BF16),
                              (((0,), (0,)), ((), ())), preferred_element_type=F32)
        yield
        s_new = s0 * jnp.exp(g_last)
        for h in range(HEADS_PER_TILE):
            s_new = s_new + upd[h * RWKV_HEAD:(h + 1) * RWKV_HEAD] * mask("head%d" % h)
        return y, s_new

    d = RWKV_DIM
    first_row = lax.broadcasted_iota(jnp.int32, (c, MXU_WIDTH), 0) == 0

    def load_rows(batches, cols):
        z = [z_ref[b, :, cols].astype(F32) for b in batches]
        zs = [jnp.where(first_row[:, 0:cols.stop - cols.start], carry_ref[b, 0:1, cols],
                        pltpu.roll(zb, 1, axis=0)) for b, zb in zip(batches, z)]
        return jnp.concatenate(z, axis=0), jnp.concatenate(zs, axis=0)

    def start_unit(batches, j, shared):
        cols = slice(j * MXU_WIDTH, (j + 1) * MXU_WIDTH)
        sections = [load_rows(batches, slice(sec * d + cols.start, sec * d + cols.stop))
                    for sec in range(3)]
        prep = _rwkv_prep_tile(cols, [zz for zz, _ in sections], [zs for _, zs in sections], *shared,
                               mu_ref, w0_ref, a0_ref, kkw_ref, ka_ref, rk_ref, wa2_ref, g2_ref,
                               e_ref)
        r, lw, k, v, kk, kb, g, bonus = prep
        gens = [tile(*(t[bb * c:(bb + 1) * c] for t in (r, lw, k, v, kk, kb)), s_ref[b, :, cols])
                for bb, b in enumerate(batches)]

        def finish(results):
            for bb, b in enumerate(batches):
                y, s_new = results[bb]
                rows = slice(bb * c, (bb + 1) * c)
                g_ref[b, :, cols] = g[rows].astype(g_ref.dtype)
                bonus_ref[b, :, cols] = bonus[rows].astype(bonus_ref.dtype)
                y_ref[b, :, cols] = y.astype(y_ref.dtype)
                s_ref[b, :, cols] = s_new

        return gens, finish

    units = []
    for g0 in range(0, bsz, RWKV_GROUP_BATCHES):
        batches = list(range(g0, g0 + RWKV_GROUP_BATCHES))
        for j in range(n_tiles):
            units.append((batches, j))
    running = []
    shared = None
    while units or running:
        for _ in range(min(RWKV_UNITS_PER_STAGE, len(units))):
            batches, j = units.pop(0)
            if j == 0:
                shared = _rwkv_prep_shared(*load_rows(batches, slice(3 * d, RWKV_COLS)), mu_ref)
            gens, finish = start_unit(batches, j, shared)
            running.append([gens, [None] * len(gens), finish])
        for entry in list(running):
            gens, results, finish = entry
            for idx, gen in enumerate(gens):
                if results[idx] is None:
                    try:
                        next(gen)
                    except StopIteration as stop:
                        results[idx] = stop.value
            if all(res is not None for res in results):
                finish(results)
                running.remove(entry)
    carry_ref[:, 0:1, :] = z_ref[:, c - 1:c, :].astype(F32)


RWKV_MASKS = ("strict", "incl", "eye", "block8", "off16", "off32", "off64",
              "head0", "head1", "head2", "head3")


def _rwkv_chunk_constants():
    c = RWKV_CHUNK
    t = np.arange(c)[:, None]
    lane = np.arange(MXU_WIDTH)[None, :]
    s = lane % c
    same = lambda size: (t // size) == (s // size)
    table = {"strict": s < t, "incl": s <= t, "eye": s == t, "block8": (s < t) & same(8)}
    for size in (16, 32, 64):
        table["off%d" % size] = (s < t) & same(size) & ~same(size // 2)
    for h in range(HEADS_PER_TILE):
        table["head%d" % h] = np.broadcast_to(lane // RWKV_HEAD == h, (c, MXU_WIDTH))
    masks = np.stack([np.broadcast_to(table[name], (c, MXU_WIDTH)) for name in RWKV_MASKS])
    tri = np.arange(c)[:, None] >= np.arange(c)[None, :]
    return jnp.asarray(masks.astype(np.float32)), jnp.asarray(tri.astype(np.float32), BF16)


def _rwkv_chunk(z_r, mu, w0, a0, k_k, k_a, r_k, wa2, g2, e_mat):
    bsz, s_len, _ = z_r.shape
    d = RWKV_DIM
    assert bsz % RWKV_GROUP_BATCHES == 0 and s_len % RWKV_CHUNK == 0
    whole = lambda a: pl.BlockSpec(a.shape, lambda i: (0,) * a.ndim)
    out_blk = pl.BlockSpec((bsz, RWKV_CHUNK, d), lambda i: (0, i, 0))
    params = (mu, w0, a0, k_k, k_a, r_k, wa2, g2, e_mat) + _rwkv_chunk_constants()
    return pl.pallas_call(
        _rwkv_chunk_kernel,
        out_shape=[jax.ShapeDtypeStruct((bsz, s_len, d), BF16)] * 3,
        grid=(s_len // RWKV_CHUNK,),
        in_specs=[pl.BlockSpec((bsz, RWKV_CHUNK, RWKV_COLS), lambda i: (0, i, 0))]
        + [whole(p) for p in params],
        out_specs=[out_blk] * 3,
        scratch_shapes=[pltpu.VMEM((bsz, RWKV_HEAD, d), F32),
                        pltpu.VMEM((bsz, SUBLANES, RWKV_COLS), F32)],
        compiler_params=_cparams(("arbitrary",)),
        name="rwkv7_chunk",
    )(z_r, *params)


def _merge_kernel(x_ref, oa_ref, y_ref, bonus_ref, g_ref, ga_ref, gb_ref, lnw_ref, lnb_ref,
                  pn_ref, wa_ref, wb_ref, wo_ref, e_ref, o_ref):
    y = y_ref[...].astype(F32)
    inv_n = 1.0 / RWKV_HEAD
    mu = _segment_sum_bcast(y, e_ref) * inv_n
    yc = y - mu
    var = _segment_sum_bcast(yc * yc, e_ref) * inv_n
    yn = yc * lax.rsqrt(var + GN_EPS) * lnw_ref[...] + lnb_ref[...]
    o_b = (yn + bonus_ref[...].astype(F32)) * g_ref[...].astype(F32)
    y_b = jnp.dot(o_b.astype(BF16), wb_ref[...], preferred_element_type=F32)
    y_a = jnp.dot(oa_ref[...].astype(BF16), wa_ref[...], preferred_element_type=F32)
    merged = ga_ref[...].astype(F32) * y_a + gb_ref[...].astype(F32) * y_b
    mix = jnp.dot(merged.astype(BF16), wo_ref[...], preferred_element_type=F32)
    ms = jnp.mean(mix * mix, axis=-1, keepdims=True)
    o_ref[...] = x_ref[...] + mix * lax.rsqrt(ms + EPS) * pn_ref[...]


def _merge(x2, o_a, y, bonus, g, gates, ln_w, ln_b, post_norm, w_a, w_b, w_o, e_mat, tm):
    m, d = x2.shape
    blk = pl.BlockSpec((tm, d), lambda i: (i, 0))
    row = pl.BlockSpec((1, d), lambda i: (0, 0))
    wsp = pl.BlockSpec((d, d), lambda i: (0, 0))
    return pl.pallas_call(
        _merge_kernel,
        out_shape=jax.ShapeDtypeStruct((m, d), F32),
        grid=(m // tm,),
        in_specs=[blk, blk, blk, blk, blk,
                  pl.BlockSpec((tm, d), lambda i: (i, 0)),
                  pl.BlockSpec((tm, d), lambda i: (i, 1)),
                  row, row, row, wsp, wsp, wsp,
                  pl.BlockSpec((MXU_WIDTH, MXU_WIDTH), lambda i: (0, 0))],
        out_specs=blk,
        compiler_params=_cparams(("arbitrary",)),
        name="merge_outproj",
    )(x2, o_a, y, bonus, g, gates, gates, ln_w, ln_b, post_norm, w_a, w_b, w_o, e_mat)


def _ffn_kernel(h_ref, pre_ref, post_ref, wup_ref, cw_ref, cb_ref, wdn_ref, o_ref,
                xn_ref, buf_ref, carry_ref, act_ref):
    tm = h_ref.shape[1]
    ct = FFN_COL_TILE
    n_tiles = D_FF // ct
    halo = SUBLANES

    @pl.when(pl.program_id(1) == 0)
    def _():
        carry_ref[...] = jnp.zeros_like(carry_ref)

    h = h_ref[0]
    ms = jnp.mean(h * h, axis=-1, keepdims=True)
    xn_ref[...] = (h * lax.rsqrt(ms + EPS) * pre_ref[...]).astype(BF16)

    def conv(c0, slot):
        cols = slice(c0, c0 + ct)
        hu = jnp.dot(xn_ref[...], wup_ref[:, cols], preferred_element_type=F32)
        buf_ref[slot, 0:halo, :] = carry_ref[:, cols]
        buf_ref[slot, halo:halo + tm, :] = hu
        carry_ref[:, cols] = hu[tm - halo:tm, :]
        cw = cw_ref[:, cols]
        out = cb_ref[:, cols] + cw[CONV_W - 1:CONV_W, :] * hu
        for j in range(CONV_W - 1):
            back = CONV_W - 1 - j
            out = out + cw[j:j + 1, :] * buf_ref[slot, halo - back:halo - back + tm, :]
        return out

    for i in range(n_tiles):
        gate = conv(i * ct, 0)
        val = conv(D_FF + i * ct, 1)
        act_ref[:, i * ct:(i + 1) * ct] = (_silu(gate) * val).astype(BF16)

    ff = jnp.dot(act_ref[...], wdn_ref[...], preferred_element_type=F32)
    ms2 = jnp.mean(ff * ff, axis=-1, keepdims=True)
    o_ref[0] = h + ff * lax.rsqrt(ms2 + EPS) * post_ref[...]


def _ffn(h1, pre, post, w_up, conv_w, conv_b, w_down, tm):
    bsz, s_len, d = h1.shape
    blk = pl.BlockSpec((1, tm, d), lambda b, i: (b, i, 0))
    row = pl.BlockSpec((1, d), lambda b, i: (0, 0))
    return pl.pallas_call(
        _ffn_kernel,
        out_shape=jax.ShapeDtypeStruct((bsz, s_len, d), F32),
        grid=(bsz, s_len // tm),
        in_specs=[blk, row, row,
                  pl.BlockSpec((d, 2 * D_FF), lambda b, i: (0, 0)),
                  pl.BlockSpec((CONV_W, 2 * D_FF), lambda b, i: (0, 0)),
                  pl.BlockSpec((1, 2 * D_FF), lambda b, i: (0, 0)),
                  pl.BlockSpec((D_FF, d), lambda b, i: (0, 0))],
        out_specs=blk,
        scratch_shapes=[
            pltpu.VMEM((tm, d), BF16),
            pltpu.VMEM((2, tm + SUBLANES, FFN_COL_TILE), F32),
            pltpu.VMEM((SUBLANES, 2 * D_FF), F32),
            pltpu.VMEM((tm, D_FF), BF16),
        ],
        compiler_params=_cparams(("arbitrary", "arbitrary")),
        name="conv_ffn",
    )(h1, pre, post, w_up, conv_w, conv_b, w_down)


def _head_segment_ones():
    lane = np.arange(MXU_WIDTH)
    e_mat = (lane[:, None] // RWKV_HEAD == lane[None, :] // RWKV_HEAD).astype(np.float32)
    return jnp.asarray(e_mat, BF16)


def kernel(x, attn_pre_norm, w_in, hgrn_lb, hgrn_gnorm, w_branch_a, rwkv_mu, rwkv_w0, rwkv_w2,
           rwkv_a0, rwkv_a2, rwkv_g2, rwkv_k_k, rwkv_k_a, rwkv_r_k, rwkv_ln_w, rwkv_ln_b,
           w_branch_b, w_out, attn_post_norm, ffn_pre_norm, w_up, conv_w, conv_b, w_down,
           ffn_post_norm):
    bsz, s_len, d = x.shape
    m = bsz * s_len
    e_mat = _head_segment_ones()
    h = x
    for l in range(w_in.shape[0]):
        x2 = h.reshape(m, d)
        pre = attn_pre_norm[l].reshape(1, d)
        zq, zf, zi, zg, z_r, gates = _inproj(
            x2, pre, w_in[l].astype(BF16), tm=256,
            sections=[(HGRN_FWD, BF16, None), (HGRN_FWD, F32, None), (HGRN_IN, BF16, None),
                      (HGRN_IN, BF16, None), (RWKV_COLS, BF16, None), (GATE_COLS, BF16, "sigmoid")])

        o_a = _hgrn(*(t.reshape(bsz, s_len, -1) for t in (zq, zf, zi, zg)), hgrn_lb,
                    hgrn_gnorm[l].reshape(1, d), l, BF16)

        wa2 = jnp.zeros((W_LORA + A_LORA, 2 * d), F32)
        wa2 = wa2.at[:W_LORA, :d].set(rwkv_w2[l]).at[W_LORA:, d:].set(rwkv_a2[l]).astype(BF16)
        y, g, bonus = _rwkv_chunk(
            z_r.reshape(bsz, s_len, RWKV_COLS), rwkv_mu[l].reshape(1, -1),
            rwkv_w0[l].reshape(1, d), rwkv_a0[l].reshape(1, d), rwkv_k_k[l].reshape(1, d),
            rwkv_k_a[l].reshape(1, d), rwkv_r_k[l].reshape(1, d), wa2,
            rwkv_g2[l].astype(BF16), e_mat)

        h1 = _merge(x2, o_a.reshape(m, d), y.reshape(m, d), bonus.reshape(m, d),
                    g.reshape(m, d), gates, rwkv_ln_w[l].reshape(1, d),
                    rwkv_ln_b[l].reshape(1, d), attn_post_norm[l].reshape(1, d),
                    w_branch_a[l].astype(BF16), w_branch_b[l].astype(BF16),
                    w_out[l].astype(BF16), e_mat, tm=512)

        h = _ffn(h1.reshape(bsz, s_len, d), ffn_pre_norm[l].reshape(1, d),
                 ffn_post_norm[l].reshape(1, d), w_up[l].astype(BF16), conv_w[l],
                 conv_b[l].reshape(1, -1), w_down[l].astype(BF16), tm=512)
    return h
```

```python
import functools

import jax
import jax.numpy as jnp
import numpy as np
from jax import lax
from jax.experimental import pallas as pl
from jax.experimental.pallas import tpu as pltpu

F32 = jnp.float32
BF16 = jnp.bfloat16

D_MODEL = 1024
HGRN_HEADS = 8
HGRN_EXPAND = 128
HGRN_FWD = HGRN_HEADS * HGRN_EXPAND
HGRN_IN = D_MODEL
HGRN_SCALE = HGRN_EXPAND ** -0.5
CHUNK = 32
RWKV_HEAD = 64
RWKV_DIM = D_MODEL
W_LORA = 64
A_LORA = 64
G_LORA = 128
GN_EPS = 1e-5 * RWKV_HEAD
D_FF = 2816
CONV_W = 3
EPS = 1e-6
HGRN_COLS = 2 * HGRN_FWD + 2 * HGRN_IN
RWKV_COLS = 3 * RWKV_DIM + W_LORA + A_LORA + G_LORA
GATE_COLS = 2 * D_MODEL

SUBLANES = 8
LANES = 128
MXU_WIDTH = 256
VMEM_LIMIT_BYTES = 56 * 1024 * 1024

HGRN_CHUNK_GROUP = 8
FFN_COL_TILE = 256


def _sigmoid(x):
    return 1.0 / (1.0 + jnp.exp(-x))


def _silu(x):
    return x * _sigmoid(x)


def _bdot(a, b):
    return jnp.dot(a.astype(BF16), b.astype(BF16), preferred_element_type=F32)


def _segment_sum_bcast(x, e_ref):
    n = x.shape[-1]
    e = e_ref[...]
    outs = []
    for j in range(n // MXU_WIDTH):
        outs.append(jnp.dot(x[:, j * MXU_WIDTH:(j + 1) * MXU_WIDTH].astype(BF16), e,
                            preferred_element_type=F32))
    return jnp.concatenate(outs, axis=-1)


def _cparams(semantics):
    return pltpu.CompilerParams(dimension_semantics=semantics,
                                vmem_limit_bytes=VMEM_LIMIT_BYTES)


def _inproj_kernel(x_ref, g_ref, w_ref, *o_refs, acts):
    x = x_ref[...]
    ms = jnp.mean(x * x, axis=-1, keepdims=True)
    xn = (x * lax.rsqrt(ms + EPS) * g_ref[...]).astype(BF16)
    off = 0
    for o_ref, act in zip(o_refs, acts):
        n = o_ref.shape[1]
        z = jnp.dot(xn, w_ref[:, off:off + n], preferred_element_type=F32)
        if act == "sigmoid":
            z = _sigmoid(z)
        o_ref[...] = z.astype(o_ref.dtype)
        off += n


def _inproj(x2, gain, w, *, sections, tm):
    m, d = x2.shape
    n = w.shape[1]
    assert sum(cols for cols, _, _ in sections) == n
    return pl.pallas_call(
        functools.partial(_inproj_kernel, acts=[act for _, _, act in sections]),
        out_shape=[jax.ShapeDtypeStruct((m, cols), dt) for cols, dt, _ in sections],
        grid=(m // tm,),
        in_specs=[
            pl.BlockSpec((tm, d), lambda i: (i, 0)),
            pl.BlockSpec((1, d), lambda i: (0, 0)),
            pl.BlockSpec((d, n), lambda i: (0, 0), pipeline_mode=pl.Buffered(1)),
        ],
        out_specs=[pl.BlockSpec((tm, cols), lambda i: (i, 0)) for cols, _, _ in sections],
        compiler_params=_cparams(("arbitrary",)),
        name="inproj",
    )(x2, gain, w)


def _hgrn_kernel(hq_ref, hf_ref, hi_ref, hg_ref, lb_ref, gn_ref, o_ref, state_ref, *, layer):
    s_len = hq_ref.shape[1]
    rows = HGRN_CHUNK_GROUP * CHUNK
    n_groups = s_len // rows

    lbp = lb_ref[...]
    lbm = jnp.max(lbp, axis=0, keepdims=True)
    lbe = jnp.exp(lbp - lbm)
    lb = (jnp.sum(lbe[0:layer + 1, :], axis=0, keepdims=True)
          / jnp.sum(lbe, axis=0, keepdims=True))
    gn = gn_ref[...]

    state_ref[...] = jnp.zeros_like(state_ref)

    row_in_chunk = lax.broadcasted_iota(jnp.int32, (rows, HGRN_EXPAND), 0) % CHUNK
    ci = lax.broadcasted_iota(jnp.int32, (CHUNK, CHUNK), 0)
    di = lax.broadcasted_iota(jnp.int32, (CHUNK, CHUNK), 1)
    tril = ci >= di

    def group(gi, carry):
        r0 = pl.multiple_of(gi * rows, rows)
        hq = hq_ref[0, pl.ds(r0, rows), :].astype(F32)
        hf = hf_ref[0, pl.ds(r0, rows), :]
        v = hi_ref[0, pl.ds(r0, rows), :].astype(F32)
        hg = hg_ref[0, pl.ds(r0, rows), :].astype(F32)
        q = _silu(hq) * HGRN_SCALE
        f = lb + (1.0 - lb) * _sigmoid(hf)
        k = 1.0 - f
        b = jnp.log(f)
        sh = 1
        while sh < CHUNK:
            b = b + jnp.where(row_in_chunk >= sh, pltpu.roll(b, sh, axis=0), 0.0)
            sh *= 2
        chunks = []
        for j in range(HGRN_CHUNK_GROUP):
            sl = slice(j * CHUNK, (j + 1) * CHUNK)
            bj, qj, kj, vj = b[sl], q[sl], k[sl], v[sl]
            b_mid = bj[CHUNK // 2 - 1:CHUNK // 2, :]
            b_last = bj[CHUNK - 1:CHUNK, :]
            q_in = qj * jnp.exp(bj - b_mid)
            k_in = kj * jnp.exp(b_mid - bj)
            scores = lax.dot_general(q_in.astype(BF16), k_in.astype(BF16),
                                     (((1,), (1,)), ((), ())), preferred_element_type=F32)
            k_dec = k_in * jnp.exp(b_last - b_mid)
            u_t = _bdot(vj.T, k_dec)
            chunks.append((scores, u_t, q_in * jnp.exp(b_mid), jnp.exp(b_last), vj))
        o_intra = [_bdot(jnp.where(tril, scores, 0.0), vj) for scores, _, _, _, vj in chunks]
        outs = []
        st = state_ref[...]
        for j, (_, u_t, q_dec, decay, _) in enumerate(chunks):
            o_inter = lax.dot_general(q_dec.astype(BF16), st.astype(BF16),
                                      (((1,), (1,)), ((), ())), preferred_element_type=F32)
            st = st * decay + u_t
            outs.append(o_intra[j] + o_inter)
        state_ref[...] = st
        o = jnp.concatenate(outs, axis=0)
        o = o * lax.rsqrt(jnp.mean(o * o, axis=-1, keepdims=True) + EPS)
        o = o * gn * _silu(hg)
        o_ref[0, pl.ds(r0, rows), :] = o.astype(o_ref.dtype)
        return carry

    lax.fori_loop(0, n_groups, group, 0, unroll=8)


def _hgrn(zq, zf, zi, zg, hgrn_lb, gnorm, layer, out_dtype):
    bsz, s_len, _ = zq.shape
    col = pl.BlockSpec((1, s_len, HGRN_EXPAND), lambda b, h: (b, 0, h))

    return pl.pallas_call(
        functools.partial(_hgrn_kernel, layer=layer),
        out_shape=jax.ShapeDtypeStruct((bsz, s_len, HGRN_IN), out_dtype),
        grid=(bsz, HGRN_HEADS),
        in_specs=[
            col, col, col, col,
            pl.BlockSpec((hgrn_lb.shape[0], HGRN_EXPAND), lambda b, h: (0, h)),
            pl.BlockSpec((1, HGRN_EXPAND), lambda b, h: (0, h)),
        ],
        out_specs=pl.BlockSpec((1, s_len, HGRN_EXPAND), lambda b, h: (b, 0, h)),
        scratch_shapes=[pltpu.VMEM((HGRN_EXPAND, HGRN_EXPAND), F32)],
        compiler_params=_cparams(("arbitrary", "arbitrary")),
        name="hgrn2_chunkwise",
    )(zq, zf, zi, zg, hgrn_lb, gnorm)


RWKV_CHUNK = 64
HEADS_PER_TILE = MXU_WIDTH // RWKV_HEAD
RWKV_GROUP_BATCHES = 2
RWKV_UNITS_PER_STAGE = 1
RWKV_PROLOGUE_LEAD = 2


def _rwkv_prep_shared(z_lora, zs_lora, mu_ref):
    n = W_LORA + A_LORA
    zm = z_lora + mu_ref[:, 3 * RWKV_DIM:] * (zs_lora - z_lora)
    wa = zm[:, 0:n]
    lane = lax.broadcasted_iota(jnp.int32, wa.shape, 1)
    wa = jnp.where(lane < W_LORA, jnp.tanh(wa), wa)
    return wa.astype(BF16), _sigmoid(zm[:, n:]).astype(BF16)


def _rwkv_prep_tile(cols, z3, zs3, wa, gs, mu_ref, w0_ref, a0_ref, kkw_ref, ka_ref, rk_ref,
                    wa2_ref, g2_ref, e_ref):
    d = RWKV_DIM
    rr, kr, vr = (z + mu_ref[:, sec * d + cols.start:sec * d + cols.stop] * (zs - z)
                  for sec, (z, zs) in enumerate(zip(z3, zs3)))
    e = e_ref[...]
    seg = lambda x: jnp.dot(x.astype(BF16), e, preferred_element_type=F32)
    w_pre = w0_ref[:, cols] + jnp.dot(wa, wa2_ref[:, cols], preferred_element_type=F32)
    a_pre = a0_ref[:, cols] + jnp.dot(wa, wa2_ref[:, d + cols.start:d + cols.stop],
                                      preferred_element_type=F32)
    log_decay = -float(np.exp(-0.5)) * _sigmoid(w_pre)
    a = _sigmoid(a_pre)
    g = jnp.dot(gs, g2_ref[:, cols], preferred_element_type=F32)
    kk = kr * kkw_ref[:, cols]
    k2 = kr * (1.0 + (a - 1.0) * ka_ref[:, cols])
    rows = kk.shape[0]
    sums = seg(jnp.concatenate([kk * kk, rr * k2 * rk_ref[:, cols]], axis=0))
    kk = kk / jnp.maximum(jnp.sqrt(sums[:rows]), 1e-12)
    bonus = sums[rows:] * vr
    return rr, log_decay, k2, vr, kk, kk * a, g, bonus


def _rwkv_chunk_kernel(z_ref, mu_ref, w0_ref, a0_ref, kkw_ref, ka_ref, rk_ref, wa2_ref, g2_ref,
                       e_ref, mask_ref, tri_ref, y_ref, g_ref, bonus_ref, s_ref, carry_ref):
    bsz = z_ref.shape[0]
    c = RWKV_CHUNK
    n_tiles = RWKV_DIM // MXU_WIDTH

    @pl.when(pl.program_id(0) == 0)
    def _():
        s_ref[...] = jnp.zeros_like(s_ref)
        carry_ref[...] = jnp.zeros_like(carry_ref)

    lane = lax.broadcasted_iota(jnp.int32, (c, MXU_WIDTH), 1)
    head_masks = [lane // RWKV_HEAD == h for h in range(HEADS_PER_TILE)]

    def mask(name):
        return mask_ref[RWKV_MASKS.index(name)]

    def bd(x):
        xb = x.astype(BF16)
        zero = jnp.zeros_like(xb)
        return jnp.concatenate([jnp.where(m, xb, zero) for m in head_masks], axis=0)

    def pk(a, b_bd):
        return jnp.dot(a.astype(BF16), b_bd, preferred_element_type=F32)

    def pk_t(a, b_bd):
        return lax.dot_general(a.astype(BF16), b_bd, (((1,), (1,)), ((), ())),
                               preferred_element_type=F32)

    def tile(r, lw, k, v, kk, kb, s0):
        lw_hi = lw.astype(BF16)
        lw_lo = (lw - lw_hi.astype(F32)).astype(BF16)
        tri = tri_ref[...]
        g = (jnp.dot(tri, lw_hi, preferred_element_type=F32)
             + jnp.dot(tri, lw_lo, preferred_element_type=F32))
        g_mid = g[c // 2 - 1:c // 2, :]
        g_last = g[c - 1:c, :]
        e_neg = jnp.exp(g_mid - g)
        a_t = -kk * jnp.exp(g - lw - g_mid)
        r_t = r * jnp.exp(g - g_mid)
        b_t = kb * e_neg
        k_t = k * e_neg
        e_last = jnp.exp(g_last - g_mid)
        for _ in range(RWKV_PROLOGUE_LEAD):
            yield

        ar = jnp.concatenate([a_t, r_t], axis=0)
        nb = pk_t(ar, bd(b_t))
        nk = pk_t(ar, bd(k_t))
        x = pk_t(ar, bd(s0 * jnp.exp(g_mid)))
        yield
        n_ab = nb[0:c] * mask("strict")
        m_rb = nb[c:] * mask("incl")
        n_ak = nk[0:c] * mask("strict")
        m_rk = nk[c:] * mask("incl")

        n8 = n_ab * mask("block8")
        t_inv = mask("eye") + n8
        n_pow = pk(n8, bd(n8))
        qy = pk(jnp.concatenate([n_ak, m_rk], axis=0), bd(v))
        yield
        both = pk(jnp.concatenate([t_inv, n_pow], axis=0), bd(n_pow))
        t_inv = t_inv + both[0:c]
        n_pow = both[c:]
        yield
        t_inv = t_inv + pk(t_inv, bd(n_pow))
        yield
        for size in (16, 32, 64):
            z = pk(t_inv, bd(n_ab * mask("off%d" % size)))
            yield
            t_inv = t_inv + pk(z, bd(t_inv))
            yield

        p = pk(t_inv, bd(x[0:c] + qy[0:c]))
        yield
        y = x[c:] + qy[c:] + pk(m_rb, bd(p))
        upd = lax.dot_general(jnp.concatenate([p, v], axis=0).astype(BF16),
                              jnp.concatenate([b_t * e_last, k_t * e_last], axis=0).astype(BF16),
                              (((0,), (0,)), ((), ())), preferred_element_type=F32)
        yield
        s_new = s0 * jnp.exp(g_last)
        for h in range(HEADS_PER_TILE):
            s_new = s_new + upd[h * RWKV_HEAD:(h + 1) * RWKV_HEAD] * mask("head%d" % h)
        return y, s_new

    d = RWKV_DIM
    first_row = lax.broadcasted_iota(jnp.int32, (c, MXU_WIDTH), 0) == 0

    def load_rows(batches, cols):
        z = [z_ref[b, :, cols].astype(F32) for b in batches]
        zs = [jnp.where(first_row[:, 0:cols.stop - cols.start], carry_ref[b, 0:1, cols],
                        pltpu.roll(zb, 1, axis=0)) for b, zb in zip(batches, z)]
        return jnp.concatenate(z, axis=0), jnp.concatenate(zs, axis=0)

    def start_unit(batches, j, shared):
        cols = slice(j * MXU_WIDTH, (j + 1) * MXU_WIDTH)
        sections = [load_rows(batches, slice(sec * d + cols.start, sec * d + cols.stop))
                    for sec in range(3)]
        prep = _rwkv_prep_tile(cols, [zz for zz, _ in sections], [zs for _, zs in sections], *shared,
                               mu_ref, w0_ref, a0_ref, kkw_ref, ka_ref, rk_ref, wa2_ref, g2_ref,
                               e_ref)
        r, lw, k, v, kk, kb, g, bonus = prep
        gens = [tile(*(t[bb * c:(bb + 1) * c] for t in (r, lw, k, v, kk, kb)), s_ref[b, :, cols])
                for bb, b in enumerate(batches)]

        def finish(results):
            for bb, b in enumerate(batches):
                y, s_new = results[bb]
                rows = slice(bb * c, (bb + 1) * c)
                g_ref[b, :, cols] = g[rows].astype(g_ref.dtype)
                bonus_ref[b, :, cols] = bonus[rows].astype(bonus_ref.dtype)
                y_ref[b, :, cols] = y.astype(y_ref.dtype)
                s_ref[b, :, cols] = s_new

        return gens, finish

    units = []
    for g0 in range(0, bsz, RWKV_GROUP_BATCHES):
        batches = list(range(g0, g0 + RWKV_GROUP_BATCHES))
        for j in range(n_tiles):
            units.append((batches, j))
    running = []
    shared = None
    while units or running:
        for _ in range(min(RWKV_UNITS_PER_STAGE, len(units))):
            batches, j = units.pop(0)
            if j == 0:
                shared = _rwkv_prep_shared(*load_rows(batches, slice(3 * d, RWKV_COLS)), mu_ref)
            gens, finish = start_unit(batches, j, shared)
            running.append([gens, [None] * len(gens), finish])
        for entry in list(running):
            gens, results, finish = entry
            for idx, gen in enumerate(gens):
                if results[idx] is None:
                    try:
                        next(gen)
                    except StopIteration as stop:
                        results[idx] = stop.value
            if all(res is not None for res in results):
                finish(results)
                running.remove(entry)
    carry_ref[:, 0:1, :] = z_ref[:, c - 1:c, :].astype(F32)


RWKV_MASKS = ("strict", "incl", "eye", "block8", "off16", "off32", "off64",
              "head0", "head1", "head2", "head3")


def _rwkv_chunk_constants():
    c = RWKV_CHUNK
    t = np.arange(c)[:, None]
    lane = np.arange(MXU_WIDTH)[None, :]
    s = lane % c
    same = lambda size: (t // size) == (s // size)
    table = {"strict": s < t, "incl": s <= t, "eye": s == t, "block8": (s < t) & same(8)}
    for size in (16, 32, 64):
        table["off%d" % size] = (s < t) & same(size) & ~same(size // 2)
    for h in range(HEADS_PER_TILE):
        table["head%d" % h] = np.broadcast_to(lane // RWKV_HEAD == h, (c, MXU_WIDTH))
    masks = np.stack([np.broadcast_to(table[name], (c, MXU_WIDTH)) for name in RWKV_MASKS])
    tri = np.arange(c)[:, None] >= np.arange(c)[None, :]
    return jnp.asarray(masks.astype(np.float32)), jnp.asarray(tri.astype(np.float32), BF16)


def _rwkv_chunk(z_r, mu, w0, a0, k_k, k_a, r_k, wa2, g2, e_mat):
    bsz, s_len, _ = z_r.shape
    d = RWKV_DIM
    assert bsz % RWKV_GROUP_BATCHES == 0 and s_len % RWKV_CHUNK == 0
    whole = lambda a: pl.BlockSpec(a.shape, lambda i: (0,) * a.ndim)
    out_blk = pl.BlockSpec((bsz, RWKV_CHUNK, d), lambda i: (0, i, 0))
    params = (mu, w0, a0, k_k, k_a, r_k, wa2, g2, e_mat) + _rwkv_chunk_constants()
    return pl.pallas_call(
        _rwkv_chunk_kernel,
        out_shape=[jax.ShapeDtypeStruct((bsz, s_len, d), BF16)] * 3,
        grid=(s_len // RWKV_CHUNK,),
        in_specs=[pl.BlockSpec((bsz, RWKV_CHUNK, RWKV_COLS), lambda i: (0, i, 0))]
        + [whole(p) for p in params],
        out_specs=[out_blk] * 3,
        scratch_shapes=[pltpu.VMEM((bsz, RWKV_HEAD, d), F32),
                        pltpu.VMEM((bsz, SUBLANES, RWKV_COLS), F32)],
        compiler_params=_cparams(("arbitrary",)),
        name="rwkv7_chunk",
    )(z_r, *params)


def _merge_kernel(x_ref, oa_ref, y_ref, bonus_ref, g_ref, ga_ref, gb_ref, lnw_ref, lnb_ref,
                  pn_ref, wa_ref, wb_ref, wo_ref, e_ref, o_ref):
    y = y_ref[...].astype(F32)
    inv_n = 1.0 / RWKV_HEAD
    mu = _segment_sum_bcast(y, e_ref) * inv_n
    yc = y - mu
    var = _segment_sum_bcast(yc * yc, e_ref) * inv_n
    yn = yc * lax.rsqrt(var + GN_EPS) * lnw_ref[...] + lnb_ref[...]
    o_b = (yn + bonus_ref[...].astype(F32)) * g_ref[...].astype(F32)
    y_b = jnp.dot(o_b.astype(BF16), wb_ref[...], preferred_element_type=F32)
    y_a = jnp.dot(oa_ref[...].astype(BF16), wa_ref[...], preferred_element_type=F32)
    merged = ga_ref[...].astype(F32) * y_a + gb_ref[...].astype(F32) * y_b
    mix = jnp.dot(merged.astype(BF16), wo_ref[...], preferred_element_type=F32)
    ms = jnp.mean(mix * mix, axis=-1, keepdims=True)
    o_ref[...] = x_ref[...] + mix * lax.rsqrt(ms + EPS) * pn_ref[...]


def _merge(x2, o_a, y, bonus, g, gates, ln_w, ln_b, post_norm, w_a, w_b, w_o, e_mat, tm):
    m, d = x2.shape
    blk = pl.BlockSpec((tm, d), lambda i: (i, 0))
    row = pl.BlockSpec((1, d), lambda i: (0, 0))
    wsp = pl.BlockSpec((d, d), lambda i: (0, 0))
    return pl.pallas_call(
        _merge_kernel,
        out_shape=jax.ShapeDtypeStruct((m, d), F32),
        grid=(m // tm,),
        in_specs=[blk, blk, blk, blk, blk,
                  pl.BlockSpec((tm, d), lambda i: (i, 0)),
                  pl.BlockSpec((tm, d), lambda i: (i, 1)),
                  row, row, row, wsp, wsp, wsp,
                  pl.BlockSpec((MXU_WIDTH, MXU_WIDTH), lambda i: (0, 0))],
        out_specs=blk,
        compiler_params=_cparams(("arbitrary",)),
        name="merge_outproj",
    )(x2, o_a, y, bonus, g, gates, gates, ln_w, ln_b, post_norm, w_a, w_b, w_o, e_mat)


def _ffn_kernel(h_ref, pre_ref, post_ref, wup_ref, cw_ref, cb_ref, wdn_ref, o_ref,
                xn_ref, buf_ref, carry_ref, act_ref):
    tm = h_ref.shape[1]
    ct = FFN_COL_TILE
    n_tiles = D_FF // ct
    halo = SUBLANES

    @pl.when(pl.program_id(1) == 0)
    def _():
        carry_ref[...] = jnp.zeros_like(carry_ref)

    h = h_ref[0]
    ms = jnp.mean(h * h, axis=-1, keepdims=True)
    xn_ref[...] = (h * lax.rsqrt(ms + EPS) * pre_ref[...]).astype(BF16)

    def conv(c0, slot):
        cols = slice(c0, c0 + ct)
        hu = jnp.dot(xn_ref[...], wup_ref[:, cols], preferred_element_type=F32)
        buf_ref[slot, 0:halo, :] = carry_ref[:, cols]
        buf_ref[slot, halo:halo + tm, :] = hu
        carry_ref[:, cols] = hu[tm - halo:tm, :]
        cw = cw_ref[:, cols]
        out = cb_ref[:, cols] + cw[CONV_W - 1:CONV_W, :] * hu
        for j in range(CONV_W - 1):
            back = CONV_W - 1 - j
            out = out + cw[j:j + 1, :] * buf_ref[slot, halo - back:halo - back + tm, :]
        return out

    for i in range(n_tiles):
        gate = conv(i * ct, 0)
        val = conv(D_FF + i * ct, 1)
        act_ref[:, i * ct:(i + 1) * ct] = (_silu(gate) * val).astype(BF16)

    ff = jnp.dot(act_ref[...], wdn_ref[...], preferred_element_type=F32)
    ms2 = jnp.mean(ff * ff, axis=-1, keepdims=True)
    o_ref[0] = h + ff * lax.rsqrt(ms2 + EPS) * post_ref[...]


def _ffn(h1, pre, post, w_up, conv_w, conv_b, w_down, tm):
    bsz, s_len, d = h1.shape
    blk = pl.BlockSpec((1, tm, d), lambda b, i: (b, i, 0))
    row = pl.BlockSpec((1, d), lambda b, i: (0, 0))
    return pl.pallas_call(
        _ffn_kernel,
        out_shape=jax.ShapeDtypeStruct((bsz, s_len, d), F32),
        grid=(bsz, s_len // tm),
        in_specs=[blk, row, row,
                  pl.BlockSpec((d, 2 * D_FF), lambda b, i: (0, 0)),
                  pl.BlockSpec((CONV_W, 2 * D_FF), lambda b, i: (0, 0)),
                  pl.BlockSpec((1, 2 * D_FF), lambda b, i: (0, 0)),
                  pl.BlockSpec((D_FF, d), lambda b, i: (0, 0))],
        out_specs=blk,
        scratch_shapes=[
            pltpu.VMEM((tm, d), BF16),
            pltpu.VMEM((2, tm + SUBLANES, FFN_COL_TILE), F32),
            pltpu.VMEM((SUBLANES, 2 * D_FF), F32),
            pltpu.VMEM((tm, D_FF), BF16),
        ],
        compiler_params=_cparams(("arbitrary", "arbitrary")),
        name="conv_ffn",
    )(h1, pre, post, w_up, conv_w, conv_b, w_down)


def _head_segment_ones():
    lane = np.arange(MXU_WIDTH)
    e_mat = (lane[:, None] // RWKV_HEAD == lane[None, :] // RWKV_HEAD).astype(np.float32)
    return jnp.asarray(e_mat, BF16)


def kernel(x, attn_pre_norm, w_in, hgrn_lb, hgrn_gnorm, w_branch_a, rwkv_mu, rwkv_w0, rwkv_w2,
           rwkv_a0, rwkv_a2, rwkv_g2, rwkv_k_k, rwkv_k_a, rwkv_r_k, rwkv_ln_w, rwkv_ln_b,
           w_branch_b, w_out, attn_post_norm, ffn_pre_norm, w_up, conv_w, conv_b, w_down,
           ffn_post_norm):
    bsz, s_len, d = x.shape
    m = bsz * s_len
    e_mat = _head_segment_ones()
    h = x
    for l in range(w_in.shape[0]):
        x2 = h.reshape(m, d)
        pre = attn_pre_norm[l].reshape(1, d)
        zq, zf, zi, zg, z_r, gates = _inproj(
            x2, pre, w_in[l].astype(BF16), tm=256,
            sections=[(HGRN_FWD, BF16, None), (HGRN_FWD, F32, None), (HGRN_IN, BF16, None),
                      (HGRN_IN, BF16, None), (RWKV_COLS, BF16, None), (GATE_COLS, BF16, "sigmoid")])

        o_a = _hgrn(*(t.reshape(bsz, s_len, -1) for t in (zq, zf, zi, zg)), hgrn_lb,
                    hgrn_gnorm[l].reshape(1, d), l, BF16)

        wa2 = jnp.zeros((W_LORA + A_LORA, 2 * d), F32)
        wa2 = wa2.at[:W_LORA, :d].set(rwkv_w2[l]).at[W_LORA:, d:].set(rwkv_a2[l]).astype(BF16)
        y, g, bonus = _rwkv_chunk(
            z_r.reshape(bsz, s_len, RWKV_COLS), rwkv_mu[l].reshape(1, -1),
            rwkv_w0[l].reshape(1, d), rwkv_a0[l].reshape(1, d), rwkv_k_k[l].reshape(1, d),
            rwkv_k_a[l].reshape(1, d), rwkv_r_k[l].reshape(1, d), wa2,
            rwkv_g2[l].astype(BF16), e_mat)

        h1 = _merge(x2, o_a.reshape(m, d), y.reshape(m, d), bonus.reshape(m, d),
                    g.reshape(m, d), gates, rwkv_ln_w[l].reshape(1, d),
                    rwkv_ln_b[l].reshape(1, d), attn_post_norm[l].reshape(1, d),
                    w_branch_a[l].astype(BF16), w_branch_b[l].astype(BF16),
                    w_out[l].astype(BF16), e_mat, tm=512)

        h = _ffn(h1.reshape(bsz, s_len, d), ffn_pre_norm[l].reshape(1, d),
                 ffn_post_norm[l].reshape(1, d), w_up[l].astype(BF16), conv_w[l],
                 conv_b[l].reshape(1, -1), w_down[l].astype(BF16), tm=512)
    return h
```

```python
import functools

import jax
import jax.numpy as jnp
import numpy as np
from jax import lax
from jax.experimental import pallas as pl
from jax.experimental.pallas import tpu as pltpu

F32 = jnp.float32
BF16 = jnp.bfloat16

D_MODEL = 1024
HGRN_HEADS = 8
HGRN_EXPAND = 128
HGRN_FWD = HGRN_HEADS * HGRN_EXPAND
HGRN_IN = D_MODEL
HGRN_SCALE = HGRN_EXPAND ** -0.5
CHUNK = 32
RWKV_HEAD = 64
RWKV_DIM = D_MODEL
W_LORA = 64
A_LORA = 64
G_LORA = 128
GN_EPS = 1e-5 * RWKV_HEAD
D_FF = 2816
CONV_W = 3
EPS = 1e-6
HGRN_COLS = 2 * HGRN_FWD + 2 * HGRN_IN
RWKV_COLS = 3 * RWKV_DIM + W_LORA + A_LORA + G_LORA
GATE_COLS = 2 * D_MODEL

SUBLANES = 8
LANES = 128
MXU_WIDTH = 256
VMEM_LIMIT_BYTES = 56 * 1024 * 1024

HGRN_CHUNK_GROUP = 8
HGRN_GATE_LEAD = 2
FFN_COL_TILE = 256


def _sigmoid(x):
    return 1.0 / (1.0 + jnp.exp(-x))


def _silu(x):
    return x * _sigmoid(x)


def _bdot(a, b):
    return jnp.dot(a.astype(BF16), b.astype(BF16), preferred_element_type=F32)


def _segment_sum_bcast(x, e_ref):
    n = x.shape[-1]
    e = e_ref[...]
    outs = []
    for j in range(n // MXU_WIDTH):
        outs.append(jnp.dot(x[:, j * MXU_WIDTH:(j + 1) * MXU_WIDTH].astype(BF16), e,
                            preferred_element_type=F32))
    return jnp.concatenate(outs, axis=-1)


def _cparams(semantics):
    return pltpu.CompilerParams(dimension_semantics=semantics,
                                vmem_limit_bytes=VMEM_LIMIT_BYTES)


def _inproj_kernel(x_ref, g_ref, w_ref, *o_refs, acts):
    x = x_ref[...]
    ms = jnp.mean(x * x, axis=-1, keepdims=True)
    xn = (x * lax.rsqrt(ms + EPS) * g_ref[...]).astype(BF16)
    off = 0
    for o_ref, act in zip(o_refs, acts):
        n = o_ref.shape[1]
        z = jnp.dot(xn, w_ref[:, off:off + n], preferred_element_type=F32)
        if act == "sigmoid":
            z = _sigmoid(z)
        o_ref[...] = z.astype(o_ref.dtype)
        off += n


def _inproj(x2, gain, w, *, sections, tm):
    m, d = x2.shape
    n = w.shape[1]
    assert sum(cols for cols, _, _ in sections) == n
    return pl.pallas_call(
        functools.partial(_inproj_kernel, acts=[act for _, _, act in sections]),
        out_shape=[jax.ShapeDtypeStruct((m, cols), dt) for cols, dt, _ in sections],
        grid=(m // tm,),
        in_specs=[
            pl.BlockSpec((tm, d), lambda i: (i, 0)),
            pl.BlockSpec((1, d), lambda i: (0, 0)),
            pl.BlockSpec((d, n), lambda i: (0, 0), pipeline_mode=pl.Buffered(1)),
        ],
        out_specs=[pl.BlockSpec((tm, cols), lambda i: (i, 0)) for cols, _, _ in sections],
        compiler_params=_cparams(("arbitrary",)),
        name="inproj",
    )(x2, gain, w)


def _hgrn_kernel(hq_ref, hf_ref, hi_ref, hg_ref, lb_ref, gn_ref, tri_ref, o_ref, *, layer):
    s_len = hq_ref.shape[1]
    rows = HGRN_CHUNK_GROUP * CHUNK
    n_groups = s_len // rows

    lbp = lb_ref[...]
    lbm = jnp.max(lbp, axis=0, keepdims=True)
    lbe = jnp.exp(lbp - lbm)
    lb = (jnp.sum(lbe[0:layer + 1, :], axis=0, keepdims=True)
          / jnp.sum(lbe, axis=0, keepdims=True))
    gn = gn_ref[...]

    ci = lax.broadcasted_iota(jnp.int32, (CHUNK, CHUNK), 0)
    di = lax.broadcasted_iota(jnp.int32, (CHUNK, CHUNK), 1)
    tril = ci >= di

    def gates(g):
        sl = slice(g * rows, (g + 1) * rows)
        q = _silu(hq_ref[0, sl, :].astype(F32)) * HGRN_SCALE
        f = lb + (1.0 - lb) * _sigmoid(hf_ref[0, sl, :])
        logf = jnp.log(f)
        logf_hi = logf.astype(BF16)
        logf_lo = (logf - logf_hi.astype(F32)).astype(BF16)
        b2 = jnp.dot(tri_ref[...], jnp.concatenate([logf_hi, logf_lo], axis=1),
                     preferred_element_type=F32)
        return q, 1.0 - f, b2[:, :HGRN_EXPAND] + b2[:, HGRN_EXPAND:]

    def recur(g, q, k, b, st):
        sl_g = slice(g * rows, (g + 1) * rows)
        v = hi_ref[0, sl_g, :]
        chunks = []
        for j in range(HGRN_CHUNK_GROUP):
            sl = slice(j * CHUNK, (j + 1) * CHUNK)
            bj, qj, kj, vj = b[sl], q[sl], k[sl], v[sl]
            b_mid = bj[CHUNK // 2 - 1:CHUNK // 2, :]
            b_last = bj[CHUNK - 1:CHUNK, :]
            e_pos = jnp.exp(bj - b_mid)
            q_in = qj * e_pos
            k_in = kj / e_pos
            scores = lax.dot_general(q_in.astype(BF16), k_in.astype(BF16),
                                     (((1,), (1,)), ((), ())), preferred_element_type=F32)
            k_dec = k_in * jnp.exp(b_last - b_mid)
            u_t = lax.dot_general(vj, k_dec.astype(BF16), (((0,), (0,)), ((), ())),
                                  preferred_element_type=F32)
            chunks.append((scores, u_t, q_in * jnp.exp(b_mid), jnp.exp(b_last), vj))
        o_intra = [_bdot(jnp.where(tril, scores, 0.0), vj) for scores, _, _, _, vj in chunks]
        outs = []
        for j, (_, u_t, q_dec, decay, _) in enumerate(chunks):
            o_inter = lax.dot_general(q_dec.astype(BF16), st.astype(BF16),
                                      (((1,), (1,)), ((), ())), preferred_element_type=F32)
            st = st * decay + u_t
            outs.append(o_intra[j] + o_inter)
        o = jnp.concatenate(outs, axis=0)
        o = o * lax.rsqrt(jnp.mean(o * o, axis=-1, keepdims=True) + EPS)
        o = o * gn * _silu(hg_ref[0, sl_g, :].astype(F32))
        o_ref[0, sl_g, :] = o.astype(o_ref.dtype)
        return st

    st = jnp.zeros((HGRN_EXPAND, HGRN_EXPAND), F32)
    ready = [gates(g) for g in range(min(HGRN_GATE_LEAD, n_groups))]
    for g in range(n_groups):
        if g + HGRN_GATE_LEAD < n_groups:
            ready.append(gates(g + HGRN_GATE_LEAD))
        st = recur(g, *ready.pop(0), st)


def _hgrn(zq, zf, zi, zg, hgrn_lb, gnorm, layer, out_dtype):
    bsz, s_len, _ = zq.shape
    col = pl.BlockSpec((1, s_len, HGRN_EXPAND), lambda b, h: (b, 0, h))
    rows = HGRN_CHUNK_GROUP * CHUNK
    t = np.arange(rows)
    tri = (t[:, None] >= t[None, :]) & (t[:, None] // CHUNK == t[None, :] // CHUNK)

    return pl.pallas_call(
        functools.partial(_hgrn_kernel, layer=layer),
        out_shape=jax.ShapeDtypeStruct((bsz, s_len, HGRN_IN), out_dtype),
        grid=(bsz, HGRN_HEADS),
        in_specs=[
            col, col, col, col,
            pl.BlockSpec((hgrn_lb.shape[0], HGRN_EXPAND), lambda b, h: (0, h)),
            pl.BlockSpec((1, HGRN_EXPAND), lambda b, h: (0, h)),
            pl.BlockSpec((rows, rows), lambda b, h: (0, 0)),
        ],
        out_specs=pl.BlockSpec((1, s_len, HGRN_EXPAND), lambda b, h: (b, 0, h)),
        compiler_params=_cparams(("arbitrary", "arbitrary")),
        name="hgrn2_chunkwise",
    )(zq, zf, zi, zg, hgrn_lb, gnorm, jnp.asarray(tri.astype(np.float32), BF16))


RWKV_CHUNK = 64
HEADS_PER_TILE = MXU_WIDTH // RWKV_HEAD
RWKV_GROUP_BATCHES = 2
RWKV_UNITS_PER_STAGE = 1
RWKV_PROLOGUE_LEAD = 2


def _rwkv_prep_shared(z_lora, zs_lora, mu_ref):
    n = W_LORA + A_LORA
    zm = z_lora + mu_ref[:, 3 * RWKV_DIM:] * (zs_lora - z_lora)
    wa = zm[:, 0:n]
    lane = lax.broadcasted_iota(jnp.int32, wa.shape, 1)
    wa = jnp.where(lane < W_LORA, jnp.tanh(wa), wa)
    return wa.astype(BF16), _sigmoid(zm[:, n:]).astype(BF16)


def _rwkv_prep_tile(cols, z3, zs3, wa, gs, mu_ref, w0_ref, a0_ref, kkw_ref, ka_ref, rk_ref,
                    wa2_ref, g2_ref, e_ref):
    d = RWKV_DIM
    rr, kr, vr = (z + mu_ref[:, sec * d + cols.start:sec * d + cols.stop] * (zs - z)
                  for sec, (z, zs) in enumerate(zip(z3, zs3)))
    e = e_ref[...]
    seg = lambda x: jnp.dot(x.astype(BF16), e, preferred_element_type=F32)
    w_pre = w0_ref[:, cols] + jnp.dot(wa, wa2_ref[:, cols], preferred_element_type=F32)
    a_pre = a0_ref[:, cols] + jnp.dot(wa, wa2_ref[:, d + cols.start:d + cols.stop],
                                      preferred_element_type=F32)
    log_decay = -float(np.exp(-0.5)) * _sigmoid(w_pre)
    a = _sigmoid(a_pre)
    g = jnp.dot(gs, g2_ref[:, cols], preferred_element_type=F32)
    kk = kr * kkw_ref[:, cols]
    k2 = kr * (1.0 + (a - 1.0) * ka_ref[:, cols])
    rows = kk.shape[0]
    sums = seg(jnp.concatenate([kk * kk, rr * k2 * rk_ref[:, cols]], axis=0))
    kk = kk / jnp.maximum(jnp.sqrt(sums[:rows]), 1e-12)
    bonus = sums[rows:] * vr
    return rr, log_decay, k2, vr, kk, kk * a, g, bonus


def _rwkv_chunk_kernel(z_ref, mu_ref, w0_ref, a0_ref, kkw_ref, ka_ref, rk_ref, wa2_ref, g2_ref,
                       e_ref, mask_ref, tri_ref, y_ref, g_ref, bonus_ref, s_ref, carry_ref):
    bsz = z_ref.shape[0]
    c = RWKV_CHUNK
    n_tiles = RWKV_DIM // MXU_WIDTH

    @pl.when(pl.program_id(0) == 0)
    def _():
        s_ref[...] = jnp.zeros_like(s_ref)
        carry_ref[...] = jnp.zeros_like(carry_ref)

    lane = lax.broadcasted_iota(jnp.int32, (c, MXU_WIDTH), 1)
    head_masks = [lane // RWKV_HEAD == h for h in range(HEADS_PER_TILE)]

    def mask(name):
        return mask_ref[RWKV_MASKS.index(name)]

    def bd(x):
        xb = x.astype(BF16)
        zero = jnp.zeros_like(xb)
        return jnp.concatenate([jnp.where(m, xb, zero) for m in head_masks], axis=0)

    def pk(a, b_bd):
        return jnp.dot(a.astype(BF16), b_bd, preferred_element_type=F32)

    def pk_t(a, b_bd):
        return lax.dot_general(a.astype(BF16), b_bd, (((1,), (1,)), ((), ())),
                               preferred_element_type=F32)

    def tile(r, lw, k, v, kk, kb, s0):
        lw_hi = lw.astype(BF16)
        lw_lo = (lw - lw_hi.astype(F32)).astype(BF16)
        tri = tri_ref[...]
        g = (jnp.dot(tri, lw_hi, preferred_element_type=F32)
             + jnp.dot(tri, lw_lo, preferred_element_type=F32))
        g_mid = g[c // 2 - 1:c // 2, :]
        g_last = g[c - 1:c, :]
        e_neg = jnp.exp(g_mid - g)
        a_t = -kk * jnp.exp(g - lw - g_mid)
        r_t = r * jnp.exp(g - g_mid)
        b_t = kb * e_neg
        k_t = k * e_neg
        e_last = jnp.exp(g_last - g_mid)
        for _ in range(RWKV_PROLOGUE_LEAD):
            yield

        ar = jnp.concatenate([a_t, r_t], axis=0)
        nb = pk_t(ar, bd(b_t))
        nk = pk_t(ar, bd(k_t))
        x = pk_t(ar, bd(s0 * jnp.exp(g_mid)))
        yield
        n_ab = nb[0:c] * mask("strict")
        m_rb = nb[c:] * mask("incl")
        n_ak = nk[0:c] * mask("strict")
        m_rk = nk[c:] * mask("incl")

        n8 = n_ab * mask("block8")
        t_inv = mask("eye") + n8
        n_pow = pk(n8, bd(n8))
        qy = pk(jnp.concatenate([n_ak, m_rk], axis=0), bd(v))
        yield
        both = pk(jnp.concatenate([t_inv, n_pow], axis=0), bd(n_pow))
        t_inv = t_inv + both[0:c]
        n_pow = both[c:]
        yield
        t_inv = t_inv + pk(t_inv, bd(n_pow))
        yield
        for size in (16, 32, 64):
            z = pk(t_inv, bd(n_ab * mask("off%d" % size)))
            yield
            t_inv = t_inv + pk(z, bd(t_inv))
            yield

        p = pk(t_inv, bd(x[0:c] + qy[0:c]))
        yield
        y = x[c:] + qy[c:] + pk(m_rb, bd(p))
        upd = lax.dot_general(jnp.concatenate([p, v], axis=0).astype(BF16),
                              jnp.concatenate([b_t * e_last, k_t * e_last], axis=0).astype(BF16),
                              (((0,), (0,)), ((), ())), preferred_element_type=F32)
        yield
        s_new = s0 * jnp.exp(g_last)
        for h in range(HEADS_PER_TILE):
            s_new = s_new + upd[h * RWKV_HEAD:(h + 1) * RWKV_HEAD] * mask("head%d" % h)
        return y, s_new

    d = RWKV_DIM
    first_row = lax.broadcasted_iota(jnp.int32, (c, MXU_WIDTH), 0) == 0

    def load_rows(batches, cols):
        z = [z_ref[b, :, cols].astype(F32) for b in batches]
        zs = [jnp.where(first_row[:, 0:cols.stop - cols.start], carry_ref[b, 0:1, cols],
                        pltpu.roll(zb, 1, axis=0)) for b, zb in zip(batches, z)]
        return jnp.concatenate(z, axis=0), jnp.concatenate(zs, axis=0)

    def start_unit(batches, j, shared):
        cols = slice(j * MXU_WIDTH, (j + 1) * MXU_WIDTH)
        sections = [load_rows(batches, slice(sec * d + cols.start, sec * d + cols.stop))
                    for sec in range(3)]
        prep = _rwkv_prep_tile(cols, [zz for zz, _ in sections], [zs for _, zs in sections], *shared,
                               mu_ref, w0_ref, a0_ref, kkw_ref, ka_ref, rk_ref, wa2_ref, g2_ref,
                               e_ref)
        r, lw, k, v, kk, kb, g, bonus = prep
        gens = [tile(*(t[bb * c:(bb + 1) * c] for t in (r, lw, k, v, kk, kb)), s_ref[b, :, cols])
                for bb, b in enumerate(batches)]

        def finish(results):
            for bb, b in enumerate(batches):
                y, s_new = results[bb]
                rows = slice(bb * c, (bb + 1) * c)
                g_ref[b, :, cols] = g[rows].astype(g_ref.dtype)
                bonus_ref[b, :, cols] = bonus[rows].astype(bonus_ref.dtype)
                y_ref[b, :, cols] = y.astype(y_ref.dtype)
                s_ref[b, :, cols] = s_new

        return gens, finish

    units = []
    for g0 in range(0, bsz, RWKV_GROUP_BATCHES):
        batches = list(range(g0, g0 + RWKV_GROUP_BATCHES))
        for j in range(n_tiles):
            units.append((batches, j))
    running = []
    shared = None
    while units or running:
        for _ in range(min(RWKV_UNITS_PER_STAGE, len(units))):
            batches, j = units.pop(0)
            if j == 0:
                shared = _rwkv_prep_shared(*load_rows(batches, slice(3 * d, RWKV_COLS)), mu_ref)
            gens, finish = start_unit(batches, j, shared)
            running.append([gens, [None] * len(gens), finish])
        for entry in list(running):
            gens, results, finish = entry
            for idx, gen in enumerate(gens):
                if results[idx] is None:
                    try:
                        next(gen)
                    except StopIteration as stop:
                        results[idx] = stop.value
            if all(res is not None for res in results):
                finish(results)
                running.remove(entry)
    carry_ref[:, 0:1, :] = z_ref[:, c - 1:c, :].astype(F32)


RWKV_MASKS = ("strict", "incl", "eye", "block8", "off16", "off32", "off64",
              "head0", "head1", "head2", "head3")


def _rwkv_chunk_constants():
    c = RWKV_CHUNK
    t = np.arange(c)[:, None]
    lane = np.arange(MXU_WIDTH)[None, :]
    s = lane % c
    same = lambda size: (t // size) == (s // size)
    table = {"strict": s < t, "incl": s <= t, "eye": s == t, "block8": (s < t) & same(8)}
    for size in (16, 32, 64):
        table["off%d" % size] = (s < t) & same(size) & ~same(size // 2)
    for h in range(HEADS_PER_TILE):
        table["head%d" % h] = np.broadcast_to(lane // RWKV_HEAD == h, (c, MXU_WIDTH))
    masks = np.stack([np.broadcast_to(table[name], (c, MXU_WIDTH)) for name in RWKV_MASKS])
    tri = np.arange(c)[:, None] >= np.arange(c)[None, :]
    return jnp.asarray(masks.astype(np.float32)), jnp.asarray(tri.astype(np.float32), BF16)


def _rwkv_chunk(z_r, mu, w0, a0, k_k, k_a, r_k, wa2, g2, e_mat):
    bsz, s_len, _ = z_r.shape
    d = RWKV_DIM
    assert bsz % RWKV_GROUP_BATCHES == 0 and s_len % RWKV_CHUNK == 0
    whole = lambda a: pl.BlockSpec(a.shape, lambda i: (0,) * a.ndim)
    out_blk = pl.BlockSpec((bsz, RWKV_CHUNK, d), lambda i: (0, i, 0))
    params = (mu, w0, a0, k_k, k_a, r_k, wa2, g2, e_mat) + _rwkv_chunk_constants()
    return pl.pallas_call(
        _rwkv_chunk_kernel,
        out_shape=[jax.ShapeDtypeStruct((bsz, s_len, d), BF16)] * 3,
        grid=(s_len // RWKV_CHUNK,),
        in_specs=[pl.BlockSpec((bsz, RWKV_CHUNK, RWKV_COLS), lambda i: (0, i, 0))]
        + [whole(p) for p in params],
        out_specs=[out_blk] * 3,
        scratch_shapes=[pltpu.VMEM((bsz, RWKV_HEAD, d), F32),
                        pltpu.VMEM((bsz, SUBLANES, RWKV_COLS), F32)],
        compiler_params=_cparams(("arbitrary",)),
        name="rwkv7_chunk",
    )(z_r, *params)


def _merge_kernel(x_ref, oa_ref, y_ref, bonus_ref, g_ref, ga_ref, gb_ref, lnw_ref, lnb_ref,
                  pn_ref, wa_ref, wb_ref, wo_ref, e_ref, o_ref):
    y = y_ref[...].astype(F32)
    inv_n = 1.0 / RWKV_HEAD
    mu = _segment_sum_bcast(y, e_ref) * inv_n
    yc = y - mu
    var = _segment_sum_bcast(yc * yc, e_ref) * inv_n
    yn = yc * lax.rsqrt(var + GN_EPS) * lnw_ref[...] + lnb_ref[...]
    o_b = (yn + bonus_ref[...].astype(F32)) * g_ref[...].astype(F32)
    y_b = jnp.dot(o_b.astype(BF16), wb_ref[...], preferred_element_type=F32)
    y_a = jnp.dot(oa_ref[...].astype(BF16), wa_ref[...], preferred_element_type=F32)
    merged = ga_ref[...].astype(F32) * y_a + gb_ref[...].astype(F32) * y_b
    mix = jnp.dot(merged.astype(BF16), wo_ref[...], preferred_element_type=F32)
    ms = jnp.mean(mix * mix, axis=-1, keepdims=True)
    o_ref[...] = x_ref[...] + mix * lax.rsqrt(ms + EPS) * pn_ref[...]


def _merge(x2, o_a, y, bonus, g, gates, ln_w, ln_b, post_norm, w_a, w_b, w_o, e_mat, tm):
    m, d = x2.shape
    blk = pl.BlockSpec((tm, d), lambda i: (i, 0))
    row = pl.BlockSpec((1, d), lambda i: (0, 0))
    wsp = pl.BlockSpec((d, d), lambda i: (0, 0))
    return pl.pallas_call(
        _merge_kernel,
        out_shape=jax.ShapeDtypeStruct((m, d), F32),
        grid=(m // tm,),
        in_specs=[blk, blk, blk, blk, blk,
                  pl.BlockSpec((tm, d), lambda i: (i, 0)),
                  pl.BlockSpec((tm, d), lambda i: (i, 1)),
                  row, row, row, wsp, wsp, wsp,
                  pl.BlockSpec((MXU_WIDTH, MXU_WIDTH), lambda i: (0, 0))],
        out_specs=blk,
        compiler_params=_cparams(("arbitrary",)),
        name="merge_outproj",
    )(x2, o_a, y, bonus, g, gates, gates, ln_w, ln_b, post_norm, w_a, w_b, w_o, e_mat)


def _ffn_kernel(h_ref, pre_ref, post_ref, wup_ref, cw_ref, cb_ref, wdn_ref, o_ref,
                xn_ref, buf_ref, carry_ref, act_ref):
    tm = h_ref.shape[1]
    ct = FFN_COL_TILE
    n_tiles = D_FF // ct
    halo = SUBLANES

    @pl.when(pl.program_id(1) == 0)
    def _():
        carry_ref[...] = jnp.zeros_like(carry_ref)

    h = h_ref[0]
    ms = jnp.mean(h * h, axis=-1, keepdims=True)
    xn_ref[...] = (h * lax.rsqrt(ms + EPS) * pre_ref[...]).astype(BF16)

    def conv(c0, slot):
        cols = slice(c0, c0 + ct)
        hu = jnp.dot(xn_ref[...], wup_ref[:, cols], preferred_element_type=F32)
        buf_ref[slot, 0:halo, :] = carry_ref[:, cols]
        buf_ref[slot, halo:halo + tm, :] = hu
        carry_ref[:, cols] = hu[tm - halo:tm, :]
        cw = cw_ref[:, cols]
        out = cb_ref[:, cols] + cw[CONV_W - 1:CONV_W, :] * hu
        for j in range(CONV_W - 1):
            back = CONV_W - 1 - j
            out = out + cw[j:j + 1, :] * buf_ref[slot, halo - back:halo - back + tm, :]
        return out

    for i in range(n_tiles):
        gate = conv(i * ct, 0)
        val = conv(D_FF + i * ct, 1)
        act_ref[:, i * ct:(i + 1) * ct] = (_silu(gate) * val).astype(BF16)

    ff = jnp.dot(act_ref[...], wdn_ref[...], preferred_element_type=F32)
    ms2 = jnp.mean(ff * ff, axis=-1, keepdims=True)
    o_ref[0] = h + ff * lax.rsqrt(ms2 + EPS) * post_ref[...]


def _ffn(h1, pre, post, w_up, conv_w, conv_b, w_down, tm):
    bsz, s_len, d = h1.shape
    blk = pl.BlockSpec((1, tm, d), lambda b, i: (b, i, 0))
    row = pl.BlockSpec((1, d), lambda b, i: (0, 0))
    return pl.pallas_call(
        _ffn_kernel,
        out_shape=jax.ShapeDtypeStruct((bsz, s_len, d), F32),
        grid=(bsz, s_len // tm),
        in_specs=[blk, row, row,
                  pl.BlockSpec((d, 2 * D_FF), lambda b, i: (0, 0)),
                  pl.BlockSpec((CONV_W, 2 * D_FF), lambda b, i: (0, 0)),
                  pl.BlockSpec((1, 2 * D_FF), lambda b, i: (0, 0)),
                  pl.BlockSpec((D_FF, d), lambda b, i: (0, 0))],
        out_specs=blk,
        scratch_shapes=[
            pltpu.VMEM((tm, d), BF16),
            pltpu.VMEM((2, tm + SUBLANES, FFN_COL_TILE), F32),
            pltpu.VMEM((SUBLANES, 2 * D_FF), F32),
            pltpu.VMEM((tm, D_FF), BF16),
        ],
        compiler_params=_cparams(("arbitrary", "arbitrary")),
        name="conv_ffn",
    )(h1, pre, post, w_up, conv_w, conv_b, w_down)


def _head_segment_ones():
    lane = np.arange(MXU_WIDTH)
    e_mat = (lane[:, None] // RWKV_HEAD == lane[None, :] // RWKV_HEAD).astype(np.float32)
    return jnp.asarray(e_mat, BF16)


def kernel(x, attn_pre_norm, w_in, hgrn_lb, hgrn_gnorm, w_branch_a, rwkv_mu, rwkv_w0, rwkv_w2,
           rwkv_a0, rwkv_a2, rwkv_g2, rwkv_k_k, rwkv_k_a, rwkv_r_k, rwkv_ln_w, rwkv_ln_b,
           w_branch_b, w_out, attn_post_norm, ffn_pre_norm, w_up, conv_w, conv_b, w_down,
           ffn_post_norm):
    bsz, s_len, d = x.shape
    m = bsz * s_len
    e_mat = _head_segment_ones()
    h = x
    for l in range(w_in.shape[0]):
        x2 = h.reshape(m, d)
        pre = attn_pre_norm[l].reshape(1, d)
        zq, zf, zi, zg, z_r, gates = _inproj(
            x2, pre, w_in[l].astype(BF16), tm=256,
            sections=[(HGRN_FWD, BF16, None), (HGRN_FWD, F32, None), (HGRN_IN, BF16, None),
                      (HGRN_IN, BF16, None), (RWKV_COLS, BF16, None), (GATE_COLS, BF16, "sigmoid")])

        o_a = _hgrn(*(t.reshape(bsz, s_len, -1) for t in (zq, zf, zi, zg)), hgrn_lb,
                    hgrn_gnorm[l].reshape(1, d), l, BF16)

        wa2 = jnp.zeros((W_LORA + A_LORA, 2 * d), F32)
        wa2 = wa2.at[:W_LORA, :d].set(rwkv_w2[l]).at[W_LORA:, d:].set(rwkv_a2[l]).astype(BF16)
        y, g, bonus = _rwkv_chunk(
            z_r.reshape(bsz, s_len, RWKV_COLS), rwkv_mu[l].reshape(1, -1),
            rwkv_w0[l].reshape(1, d), rwkv_a0[l].reshape(1, d), rwkv_k_k[l].reshape(1, d),
            rwkv_k_a[l].reshape(1, d), rwkv_r_k[l].reshape(1, d), wa2,
            rwkv_g2[l].astype(BF16), e_mat)

        h1 = _merge(x2, o_a.reshape(m, d), y.reshape(m, d), bonus.reshape(m, d),
                    g.reshape(m, d), gates, rwkv_ln_w[l].reshape(1, d),
                    rwkv_ln_b[l].reshape(1, d), attn_post_norm[l].reshape(1, d),
                    w_branch_a[l].astype(BF16), w_branch_b[l].astype(BF16),
                    w_out[l].astype(BF16), e_mat, tm=512)

        h = _ffn(h1.reshape(bsz, s_len, d), ffn_pre_norm[l].reshape(1, d),
                 ffn_post_norm[l].reshape(1, d), w_up[l].astype(BF16), conv_w[l],
                 conv_b[l].reshape(1, -1), w_down[l].astype(BF16), tm=512)
    return h
```

```python
import functools

import jax
import jax.numpy as jnp
import numpy as np
from jax import lax
from jax.experimental import pallas as pl
from jax.experimental.pallas import tpu as pltpu

F32 = jnp.float32
BF16 = jnp.bfloat16

D_MODEL = 1024
HGRN_HEADS = 8
HGRN_EXPAND = 128
HGRN_FWD = HGRN_HEADS * HGRN_EXPAND
HGRN_IN = D_MODEL
HGRN_SCALE = HGRN_EXPAND ** -0.5
CHUNK = 32
RWKV_HEAD = 64
RWKV_DIM = D_MODEL
W_LORA = 64
A_LORA = 64
G_LORA = 128
GN_EPS = 1e-5 * RWKV_HEAD
D_FF = 2816
CONV_W = 3
EPS = 1e-6
HGRN_COLS = 2 * HGRN_FWD + 2 * HGRN_IN
RWKV_COLS = 3 * RWKV_DIM + W_LORA + A_LORA + G_LORA
GATE_COLS = 2 * D_MODEL

SUBLANES = 8
BF16_SUBLANES = 16
MXU_WIDTH = 256
VMEM_LIMIT_BYTES = 56 * 1024 * 1024

INPROJ_ROWS = 256
MERGE_ROWS = 512
FFN_ROWS = 512

HGRN_CHUNK_GROUP = 8
HGRN_GATE_LEAD = 2
FFN_COL_TILE = 256


def _sigmoid(x):
    return 1.0 / (1.0 + jnp.exp(-x))


def _silu(x):
    return x * _sigmoid(x)


def _bdot(a, b):
    return jnp.dot(a.astype(BF16), b.astype(BF16), preferred_element_type=F32)


def _segment_sum_bcast(x, e_ref):
    n = x.shape[-1]
    e = e_ref[...]
    outs = []
    for j in range(n // MXU_WIDTH):
        outs.append(jnp.dot(x[:, j * MXU_WIDTH:(j + 1) * MXU_WIDTH].astype(BF16), e,
                            preferred_element_type=F32))
    return jnp.concatenate(outs, axis=-1)


def _cparams(semantics):
    return pltpu.CompilerParams(dimension_semantics=semantics,
                                vmem_limit_bytes=VMEM_LIMIT_BYTES)


def _inproj_kernel(x_ref, g_ref, w_ref, *o_refs, acts):
    x = x_ref[...]
    ms = jnp.mean(x * x, axis=-1, keepdims=True)
    xn = (x * lax.rsqrt(ms + EPS) * g_ref[...]).astype(BF16)
    off = 0
    for o_ref, act in zip(o_refs, acts):
        n = o_ref.shape[1]
        z = jnp.dot(xn, w_ref[:, off:off + n], preferred_element_type=F32)
        if act == "sigmoid":
            z = _sigmoid(z)
        o_ref[...] = z.astype(o_ref.dtype)
        off += n


def _inproj(x2, gain, w, *, sections, tm):
    m, d = x2.shape
    n = w.shape[1]
    assert sum(cols for cols, _, _ in sections) == n
    return pl.pallas_call(
        functools.partial(_inproj_kernel, acts=[act for _, _, act in sections]),
        out_shape=[jax.ShapeDtypeStruct((m, cols), dt) for cols, dt, _ in sections],
        grid=(m // tm,),
        in_specs=[
            pl.BlockSpec((tm, d), lambda i: (i, 0)),
            pl.BlockSpec((1, d), lambda i: (0, 0)),
            pl.BlockSpec((d, n), lambda i: (0, 0), pipeline_mode=pl.Buffered(1)),
        ],
        out_specs=[pl.BlockSpec((tm, cols), lambda i: (i, 0)) for cols, _, _ in sections],
        compiler_params=_cparams(("arbitrary",)),
        name="inproj",
    )(x2, gain, w)


def _hgrn_kernel(hq_ref, hf_ref, hi_ref, hg_ref, lb_ref, gn_ref, tri_ref, o_ref, *, layer):
    s_len = hq_ref.shape[1]
    rows = HGRN_CHUNK_GROUP * CHUNK
    n_groups = s_len // rows

    lbp = lb_ref[...]
    lbm = jnp.max(lbp, axis=0, keepdims=True)
    lbe = jnp.exp(lbp - lbm)
    lb = (jnp.sum(lbe[0:layer + 1, :], axis=0, keepdims=True)
          / jnp.sum(lbe, axis=0, keepdims=True))
    gn = gn_ref[...]

    ci = lax.broadcasted_iota(jnp.int32, (CHUNK, CHUNK), 0)
    di = lax.broadcasted_iota(jnp.int32, (CHUNK, CHUNK), 1)
    tril = ci >= di

    def gates(g):
        sl = slice(g * rows, (g + 1) * rows)
        q = _silu(hq_ref[0, sl, :].astype(F32)) * HGRN_SCALE
        f = lb + (1.0 - lb) * _sigmoid(hf_ref[0, sl, :])
        logf = jnp.log(f)
        logf_hi = logf.astype(BF16)
        logf_lo = (logf - logf_hi.astype(F32)).astype(BF16)
        b2 = jnp.dot(tri_ref[...], jnp.concatenate([logf_hi, logf_lo], axis=1),
                     preferred_element_type=F32)
        return q, 1.0 - f, b2[:, :HGRN_EXPAND] + b2[:, HGRN_EXPAND:]

    def recur(g, q, k, b, st):
        sl_g = slice(g * rows, (g + 1) * rows)
        v = hi_ref[0, sl_g, :]
        chunks = []
        for j in range(HGRN_CHUNK_GROUP):
            sl = slice(j * CHUNK, (j + 1) * CHUNK)
            bj, qj, kj, vj = b[sl], q[sl], k[sl], v[sl]
            b_mid = bj[CHUNK // 2 - 1:CHUNK // 2, :]
            b_last = bj[CHUNK - 1:CHUNK, :]
            e_pos = jnp.exp(bj - b_mid)
            q_in = qj * e_pos
            k_in = kj / e_pos
            scores = lax.dot_general(q_in.astype(BF16), k_in.astype(BF16),
                                     (((1,), (1,)), ((), ())), preferred_element_type=F32)
            k_dec = k_in * jnp.exp(b_last - b_mid)
            u_t = lax.dot_general(vj, k_dec.astype(BF16), (((0,), (0,)), ((), ())),
                                  preferred_element_type=F32)
            chunks.append((scores, u_t, q_in * jnp.exp(b_mid), jnp.exp(b_last), vj))
        o_intra = [_bdot(jnp.where(tril, scores, 0.0), vj) for scores, _, _, _, vj in chunks]
        outs = []
        for j, (_, u_t, q_dec, decay, _) in enumerate(chunks):
            o_inter = lax.dot_general(q_dec.astype(BF16), st.astype(BF16),
                                      (((1,), (1,)), ((), ())), preferred_element_type=F32)
            st = st * decay + u_t
            outs.append(o_intra[j] + o_inter)
        o = jnp.concatenate(outs, axis=0)
        o = o * lax.rsqrt(jnp.mean(o * o, axis=-1, keepdims=True) + EPS)
        o = o * gn * _silu(hg_ref[0, sl_g, :].astype(F32))
        o_ref[0, sl_g, :] = o.astype(o_ref.dtype)
        return st

    st = jnp.zeros((HGRN_EXPAND, HGRN_EXPAND), F32)
    ready = [gates(g) for g in range(min(HGRN_GATE_LEAD, n_groups))]
    for g in range(n_groups):
        if g + HGRN_GATE_LEAD < n_groups:
            ready.append(gates(g + HGRN_GATE_LEAD))
        st = recur(g, *ready.pop(0), st)


def _hgrn(zq, zf, zi, zg, hgrn_lb, gnorm, layer, out_dtype):
    bsz, s_len, _ = zq.shape
    col = pl.BlockSpec((1, s_len, HGRN_EXPAND), lambda b, h: (b, 0, h))
    rows = HGRN_CHUNK_GROUP * CHUNK
    t = np.arange(rows)
    tri = (t[:, None] >= t[None, :]) & (t[:, None] // CHUNK == t[None, :] // CHUNK)

    return pl.pallas_call(
        functools.partial(_hgrn_kernel, layer=layer),
        out_shape=jax.ShapeDtypeStruct((bsz, s_len, HGRN_IN), out_dtype),
        grid=(bsz, HGRN_HEADS),
        in_specs=[
            col, col, col, col,
            pl.BlockSpec((hgrn_lb.shape[0], HGRN_EXPAND), lambda b, h: (0, h)),
            pl.BlockSpec((1, HGRN_EXPAND), lambda b, h: (0, h)),
            pl.BlockSpec((rows, rows), lambda b, h: (0, 0)),
        ],
        out_specs=pl.BlockSpec((1, s_len, HGRN_EXPAND), lambda b, h: (b, 0, h)),
        compiler_params=_cparams(("arbitrary", "arbitrary")),
        name="hgrn2_chunkwise",
    )(zq, zf, zi, zg, hgrn_lb, gnorm, jnp.asarray(tri.astype(np.float32), BF16))


RWKV_CHUNK = 64
HEADS_PER_TILE = MXU_WIDTH // RWKV_HEAD
RWKV_GROUP_BATCHES = 2
RWKV_UNITS_PER_STAGE = 1
RWKV_PROLOGUE_LEAD = 2


def _rwkv_prep_shared(z_lora, zs_lora, mu_ref):
    n = W_LORA + A_LORA
    zm = z_lora + mu_ref[:, 3 * RWKV_DIM:] * (zs_lora - z_lora)
    wa = zm[:, 0:n]
    lane = lax.broadcasted_iota(jnp.int32, wa.shape, 1)
    wa = jnp.where(lane < W_LORA, jnp.tanh(wa), wa)
    return wa.astype(BF16), _sigmoid(zm[:, n:]).astype(BF16)


def _rwkv_prep_tile(cols, z3, zs3, wa, gs, mu_ref, w0_ref, a0_ref, kkw_ref, ka_ref, rk_ref,
                    wa2_ref, g2_ref, e_ref):
    d = RWKV_DIM
    rr, kr, vr = (z + mu_ref[:, sec * d + cols.start:sec * d + cols.stop] * (zs - z)
                  for sec, (z, zs) in enumerate(zip(z3, zs3)))
    e = e_ref[...]
    seg = lambda x: jnp.dot(x.astype(BF16), e, preferred_element_type=F32)
    w_pre = w0_ref[:, cols] + jnp.dot(wa, wa2_ref[:, cols], preferred_element_type=F32)
    a_pre = a0_ref[:, cols] + jnp.dot(wa, wa2_ref[:, d + cols.start:d + cols.stop],
                                      preferred_element_type=F32)
    log_decay = -float(np.exp(-0.5)) * _sigmoid(w_pre)
    a = _sigmoid(a_pre)
    g = jnp.dot(gs, g2_ref[:, cols], preferred_element_type=F32)
    kk = kr * kkw_ref[:, cols]
    k2 = kr * (1.0 + (a - 1.0) * ka_ref[:, cols])
    rows = kk.shape[0]
    sums = seg(jnp.concatenate([kk * kk, rr * k2 * rk_ref[:, cols]], axis=0))
    kk = kk / jnp.maximum(jnp.sqrt(sums[:rows]), 1e-12)
    bonus = sums[rows:] * vr
    return rr, log_decay, k2, vr, kk, kk * a, g, bonus


def _rwkv_chunk_kernel(z_ref, mu_ref, w0_ref, a0_ref, kkw_ref, ka_ref, rk_ref, wa2_ref, g2_ref,
                       e_ref, mask_ref, tri_ref, *rest, n_casts):
    cast_in, (y_ref, g_ref, bonus_ref) = rest[:n_casts], rest[n_casts:n_casts + 3]
    cast_out, (s_ref, carry_ref) = rest[n_casts + 3:2 * n_casts + 3], rest[2 * n_casts + 3:]
    for src, dst in zip(cast_in, cast_out):
        dst[...] = src[...].astype(dst.dtype)

    bsz = z_ref.shape[0]
    c = RWKV_CHUNK
    n_tiles = RWKV_DIM // MXU_WIDTH

    @pl.when(pl.program_id(0) == 0)
    def _():
        s_ref[...] = jnp.zeros_like(s_ref)
        carry_ref[...] = jnp.zeros_like(carry_ref)

    lane = lax.broadcasted_iota(jnp.int32, (c, MXU_WIDTH), 1)
    head_masks = [lane // RWKV_HEAD == h for h in range(HEADS_PER_TILE)]

    def mask(name):
        return mask_ref[RWKV_MASKS.index(name)]

    def bd(x):
        xb = x.astype(BF16)
        zero = jnp.zeros_like(xb)
        return jnp.concatenate([jnp.where(m, xb, zero) for m in head_masks], axis=0)

    def pk(a, b_bd):
        return jnp.dot(a.astype(BF16), b_bd, preferred_element_type=F32)

    def pk_t(a, b_bd):
        return lax.dot_general(a.astype(BF16), b_bd, (((1,), (1,)), ((), ())),
                               preferred_element_type=F32)

    def tile(r, lw, k, v, kk, kb, s0):
        lw_hi = lw.astype(BF16)
        lw_lo = (lw - lw_hi.astype(F32)).astype(BF16)
        tri = tri_ref[...]
        g = (jnp.dot(tri, lw_hi, preferred_element_type=F32)
             + jnp.dot(tri, lw_lo, preferred_element_type=F32))
        g_mid = g[c // 2 - 1:c // 2, :]
        g_last = g[c - 1:c, :]
        e_neg = jnp.exp(g_mid - g)
        a_t = -kk * jnp.exp(g - lw - g_mid)
        r_t = r * jnp.exp(g - g_mid)
        b_t = kb * e_neg
        k_t = k * e_neg
        e_last = jnp.exp(g_last - g_mid)
        for _ in range(RWKV_PROLOGUE_LEAD):
            yield

        ar = jnp.concatenate([a_t, r_t], axis=0)
        nb = pk_t(ar, bd(b_t))
        nk = pk_t(ar, bd(k_t))
        x = pk_t(ar, bd(s0 * jnp.exp(g_mid)))
        yield
        n_ab = nb[0:c] * mask("strict")
        m_rb = nb[c:] * mask("incl")
        n_ak = nk[0:c] * mask("strict")
        m_rk = nk[c:] * mask("incl")

        n8 = n_ab * mask("block8")
        t_inv = mask("eye") + n8
        n_pow = pk(n8, bd(n8))
        qy = pk(jnp.concatenate([n_ak, m_rk], axis=0), bd(v))
        yield
        both = pk(jnp.concatenate([t_inv, n_pow], axis=0), bd(n_pow))
        t_inv = t_inv + both[0:c]
        n_pow = both[c:]
        yield
        t_inv = t_inv + pk(t_inv, bd(n_pow))
        yield
        for size in (16, 32, 64):
            z = pk(t_inv, bd(n_ab * mask("off%d" % size)))
            yield
            t_inv = t_inv + pk(z, bd(t_inv))
            yield

        p = pk(t_inv, bd(x[0:c] + qy[0:c]))
        yield
        y = x[c:] + qy[c:] + pk(m_rb, bd(p))
        upd = lax.dot_general(jnp.concatenate([p, v], axis=0).astype(BF16),
                              jnp.concatenate([b_t * e_last, k_t * e_last], axis=0).astype(BF16),
                              (((0,), (0,)), ((), ())), preferred_element_type=F32)
        yield
        s_new = s0 * jnp.exp(g_last)
        for h in range(HEADS_PER_TILE):
            s_new = s_new + upd[h * RWKV_HEAD:(h + 1) * RWKV_HEAD] * mask("head%d" % h)
        return y, s_new

    d = RWKV_DIM
    first_row = lax.broadcasted_iota(jnp.int32, (c, MXU_WIDTH), 0) == 0

    def load_rows(batches, cols):
        z = [z_ref[b, :, cols].astype(F32) for b in batches]
        zs = [jnp.where(first_row[:, 0:cols.stop - cols.start], carry_ref[b, 0:1, cols],
                        pltpu.roll(zb, 1, axis=0)) for b, zb in zip(batches, z)]
        return jnp.concatenate(z, axis=0), jnp.concatenate(zs, axis=0)

    def start_unit(batches, j, shared):
        cols = slice(j * MXU_WIDTH, (j + 1) * MXU_WIDTH)
        sections = [load_rows(batches, slice(sec * d + cols.start, sec * d + cols.stop))
                    for sec in range(3)]
        prep = _rwkv_prep_tile(cols, [zz for zz, _ in sections], [zs for _, zs in sections], *shared,
                               mu_ref, w0_ref, a0_ref, kkw_ref, ka_ref, rk_ref, wa2_ref, g2_ref,
                               e_ref)
        r, lw, k, v, kk, kb, g, bonus = prep
        gens = [tile(*(t[bb * c:(bb + 1) * c] for t in (r, lw, k, v, kk, kb)), s_ref[b, :, cols])
                for bb, b in enumerate(batches)]

        def finish(results):
            for bb, b in enumerate(batches):
                y, s_new = results[bb]
                rows = slice(bb * c, (bb + 1) * c)
                g_ref[b, :, cols] = g[rows].astype(g_ref.dtype)
                bonus_ref[b, :, cols] = bonus[rows].astype(bonus_ref.dtype)
                y_ref[b, :, cols] = y.astype(y_ref.dtype)
                s_ref[b, :, cols] = s_new

        return gens, finish

    units = []
    for g0 in range(0, bsz, RWKV_GROUP_BATCHES):
        batches = list(range(g0, g0 + RWKV_GROUP_BATCHES))
        for j in range(n_tiles):
            units.append((batches, j))
    running = []
    shared = None
    while units or running:
        for _ in range(min(RWKV_UNITS_PER_STAGE, len(units))):
            batches, j = units.pop(0)
            if j == 0:
                shared = _rwkv_prep_shared(*load_rows(batches, slice(3 * d, RWKV_COLS)), mu_ref)
            gens, finish = start_unit(batches, j, shared)
            running.append([gens, [None] * len(gens), finish])
        for entry in list(running):
            gens, results, finish = entry
            for idx, gen in enumerate(gens):
                if results[idx] is None:
                    try:
                        next(gen)
                    except StopIteration as stop:
                        results[idx] = stop.value
            if all(res is not None for res in results):
                finish(results)
                running.remove(entry)
    carry_ref[:, 0:1, :] = z_ref[:, c - 1:c, :].astype(F32)


RWKV_MASKS = ("strict", "incl", "eye", "block8", "off16", "off32", "off64",
              "head0", "head1", "head2", "head3")


def _rwkv_chunk_constants():
    c = RWKV_CHUNK
    t = np.arange(c)[:, None]
    lane = np.arange(MXU_WIDTH)[None, :]
    s = lane % c
    same = lambda size: (t // size) == (s // size)
    table = {"strict": s < t, "incl": s <= t, "eye": s == t, "block8": (s < t) & same(8)}
    for size in (16, 32, 64):
        table["off%d" % size] = (s < t) & same(size) & ~same(size // 2)
    for h in range(HEADS_PER_TILE):
        table["head%d" % h] = np.broadcast_to(lane // RWKV_HEAD == h, (c, MXU_WIDTH))
    masks = np.stack([np.broadcast_to(table[name], (c, MXU_WIDTH)) for name in RWKV_MASKS])
    tri = np.arange(c)[:, None] >= np.arange(c)[None, :]
    return jnp.asarray(masks.astype(np.float32)), jnp.asarray(tri.astype(np.float32), BF16)


def _rwkv_chunk(z_r, mu, w0, a0, k_k, k_a, r_k, wa2, g2, e_mat, casts):
    bsz, s_len, _ = z_r.shape
    d = RWKV_DIM
    assert bsz % RWKV_GROUP_BATCHES == 0 and s_len % RWKV_CHUNK == 0
    n_steps = s_len // RWKV_CHUNK
    whole = lambda a: pl.BlockSpec(a.shape, lambda i: (0,) * a.ndim)
    out_blk = pl.BlockSpec((bsz, RWKV_CHUNK, d), lambda i: (0, i, 0))
    params = (mu, w0, a0, k_k, k_a, r_k, wa2, g2, e_mat) + _rwkv_chunk_constants()
    slab_in, slab_out, cast_shapes = [], [], []
    for w, layer in casts:
        _, rows, width = w.shape
        n_slabs = max(n for n in range(1, n_steps + 1) if (rows // BF16_SUBLANES) % n == 0)
        slab = rows // n_slabs
        slab_in.append(pl.BlockSpec(
            (None, slab, width), lambda i, layer=layer, last=n_slabs - 1: (layer, jnp.minimum(i, last), 0)))
        slab_out.append(pl.BlockSpec(
            (slab, width), lambda i, last=n_slabs - 1: (jnp.minimum(i, last), 0)))
        cast_shapes.append(jax.ShapeDtypeStruct((rows, width), BF16))
    outs = pl.pallas_call(
        functools.partial(_rwkv_chunk_kernel, n_casts=len(casts)),
        out_shape=[jax.ShapeDtypeStruct((bsz, s_len, d), BF16)] * 3 + cast_shapes,
        grid=(n_steps,),
        in_specs=[pl.BlockSpec((bsz, RWKV_CHUNK, RWKV_COLS), lambda i: (0, i, 0))]
        + [whole(p) for p in params] + slab_in,
        out_specs=[out_blk] * 3 + slab_out,
        scratch_shapes=[pltpu.VMEM((bsz, RWKV_HEAD, d), F32),
                        pltpu.VMEM((bsz, SUBLANES, RWKV_COLS), F32)],
        compiler_params=_cparams(("arbitrary",)),
        name="rwkv7_chunk",
    )(z_r, *params, *(w for w, _ in casts))
    return outs[:3], outs[3:]


def _merge_kernel(x_ref, oa_ref, y_ref, bonus_ref, g_ref, ga_ref, gb_ref, lnw_ref, lnb_ref,
                  pn_ref, wa_ref, wb_ref, wo_ref, e_ref, o_ref):
    y = y_ref[...].astype(F32)
    inv_n = 1.0 / RWKV_HEAD
    mu = _segment_sum_bcast(y, e_ref) * inv_n
    yc = y - mu
    var = _segment_sum_bcast(yc * yc, e_ref) * inv_n
    yn = yc * lax.rsqrt(var + GN_EPS) * lnw_ref[...] + lnb_ref[...]
    o_b = (yn + bonus_ref[...].astype(F32)) * g_ref[...].astype(F32)
    y_b = jnp.dot(o_b.astype(BF16), wb_ref[...], preferred_element_type=F32)
    y_a = jnp.dot(oa_ref[...].astype(BF16), wa_ref[...], preferred_element_type=F32)
    merged = ga_ref[...].astype(F32) * y_a + gb_ref[...].astype(F32) * y_b
    mix = jnp.dot(merged.astype(BF16), wo_ref[...], preferred_element_type=F32)
    ms = jnp.mean(mix * mix, axis=-1, keepdims=True)
    o_ref[...] = x_ref[...] + mix * lax.rsqrt(ms + EPS) * pn_ref[...]


def _merge(x2, o_a, y, bonus, g, gates, ln_w, ln_b, post_norm, w_a, w_b, w_o, e_mat, tm):
    m, d = x2.shape
    blk = pl.BlockSpec((tm, d), lambda i: (i, 0))
    row = pl.BlockSpec((1, d), lambda i: (0, 0))
    wsp = pl.BlockSpec((d, d), lambda i: (0, 0))
    return pl.pallas_call(
        _merge_kernel,
        out_shape=jax.ShapeDtypeStruct((m, d), F32),
        grid=(m // tm,),
        in_specs=[blk, blk, blk, blk, blk,
                  pl.BlockSpec((tm, d), lambda i: (i, 0)),
                  pl.BlockSpec((tm, d), lambda i: (i, 1)),
                  row, row, row, wsp, wsp, wsp,
                  pl.BlockSpec((MXU_WIDTH, MXU_WIDTH), lambda i: (0, 0))],
        out_specs=blk,
        compiler_params=_cparams(("arbitrary",)),
        name="merge_outproj",
    )(x2, o_a, y, bonus, g, gates, gates, ln_w, ln_b, post_norm, w_a, w_b, w_o, e_mat)


def _ffn_kernel(h_ref, pre_ref, post_ref, wup_ref, cw_ref, cb_ref, wdn_ref, o_ref,
                xn_ref, buf_ref, carry_ref, act_ref):
    tm = h_ref.shape[1]
    ct = FFN_COL_TILE
    n_tiles = D_FF // ct
    halo = SUBLANES

    @pl.when(pl.program_id(1) == 0)
    def _():
        carry_ref[...] = jnp.zeros_like(carry_ref)

    h = h_ref[0]
    ms = jnp.mean(h * h, axis=-1, keepdims=True)
    xn_ref[...] = (h * lax.rsqrt(ms + EPS) * pre_ref[...]).astype(BF16)

    def conv(c0, slot):
        cols = slice(c0, c0 + ct)
        hu = jnp.dot(xn_ref[...], wup_ref[:, cols], preferred_element_type=F32)
        buf_ref[slot, 0:halo, :] = carry_ref[:, cols]
        buf_ref[slot, halo:halo + tm, :] = hu
        carry_ref[:, cols] = hu[tm - halo:tm, :]
        cw = cw_ref[:, cols]
        out = cb_ref[:, cols] + cw[CONV_W - 1:CONV_W, :] * hu
        for j in range(CONV_W - 1):
            back = CONV_W - 1 - j
            out = out + cw[j:j + 1, :] * buf_ref[slot, halo - back:halo - back + tm, :]
        return out

    for i in range(n_tiles):
        gate = conv(i * ct, 0)
        val = conv(D_FF + i * ct, 1)
        act_ref[:, i * ct:(i + 1) * ct] = (_silu(gate) * val).astype(BF16)

    ff = jnp.dot(act_ref[...], wdn_ref[...], preferred_element_type=F32)
    ms2 = jnp.mean(ff * ff, axis=-1, keepdims=True)
    o_ref[0] = h + ff * lax.rsqrt(ms2 + EPS) * post_ref[...]


def _ffn(h1, pre, post, w_up, conv_w, conv_b, w_down, tm):
    bsz, s_len, d = h1.shape
    blk = pl.BlockSpec((1, tm, d), lambda b, i: (b, i, 0))
    row = pl.BlockSpec((1, d), lambda b, i: (0, 0))
    return pl.pallas_call(
        _ffn_kernel,
        out_shape=jax.ShapeDtypeStruct((bsz, s_len, d), F32),
        grid=(bsz, s_len // tm),
        in_specs=[blk, row, row,
                  pl.BlockSpec((d, 2 * D_FF), lambda b, i: (0, 0)),
                  pl.BlockSpec((CONV_W, 2 * D_FF), lambda b, i: (0, 0)),
                  pl.BlockSpec((1, 2 * D_FF), lambda b, i: (0, 0)),
                  pl.BlockSpec((D_FF, d), lambda b, i: (0, 0))],
        out_specs=blk,
        scratch_shapes=[
            pltpu.VMEM((tm, d), BF16),
            pltpu.VMEM((2, tm + SUBLANES, FFN_COL_TILE), F32),
            pltpu.VMEM((SUBLANES, 2 * D_FF), F32),
            pltpu.VMEM((tm, D_FF), BF16),
        ],
        compiler_params=_cparams(("arbitrary", "arbitrary")),
        name="conv_ffn",
    )(h1, pre, post, w_up, conv_w, conv_b, w_down)


def _head_segment_ones():
    lane = np.arange(MXU_WIDTH)
    e_mat = (lane[:, None] // RWKV_HEAD == lane[None, :] // RWKV_HEAD).astype(np.float32)
    return jnp.asarray(e_mat, BF16)


def kernel(x, attn_pre_norm, w_in, hgrn_lb, hgrn_gnorm, w_branch_a, rwkv_mu, rwkv_w0, rwkv_w2,
           rwkv_a0, rwkv_a2, rwkv_g2, rwkv_k_k, rwkv_k_a, rwkv_r_k, rwkv_ln_w, rwkv_ln_b,
           w_branch_b, w_out, attn_post_norm, ffn_pre_norm, w_up, conv_w, conv_b, w_down,
           ffn_post_norm):
    bsz, s_len, d = x.shape
    m = bsz * s_len
    e_mat = _head_segment_ones()
    h = x
    for l in range(w_in.shape[0]):
        x2 = h.reshape(m, d)
        pre = attn_pre_norm[l].reshape(1, d)
        zq, zf, zi, zg, z_r, gates = _inproj(
            x2, pre, w_in[l].astype(BF16), tm=INPROJ_ROWS,
            sections=[(HGRN_FWD, BF16, None), (HGRN_FWD, F32, None), (HGRN_IN, BF16, None),
                      (HGRN_IN, BF16, None), (RWKV_COLS, BF16, None), (GATE_COLS, BF16, "sigmoid")])

        o_a = _hgrn(*(t.reshape(bsz, s_len, -1) for t in (zq, zf, zi, zg)), hgrn_lb,
                    hgrn_gnorm[l].reshape(1, d), l, BF16)

        wa2 = jnp.zeros((W_LORA + A_LORA, 2 * d), F32)
        wa2 = wa2.at[:W_LORA, :d].set(rwkv_w2[l]).at[W_LORA:, d:].set(rwkv_a2[l]).astype(BF16)
        (y, g, bonus), (w_a, w_b, w_o, w_u, w_d) = _rwkv_chunk(
            z_r.reshape(bsz, s_len, RWKV_COLS), rwkv_mu[l].reshape(1, -1),
            rwkv_w0[l].reshape(1, d), rwkv_a0[l].reshape(1, d), rwkv_k_k[l].reshape(1, d),
            rwkv_k_a[l].reshape(1, d), rwkv_r_k[l].reshape(1, d), wa2,
            rwkv_g2[l].astype(BF16), e_mat,
            casts=[(w, l) for w in (w_branch_a, w_branch_b, w_out, w_up, w_down)])

        h1 = _merge(x2, o_a.reshape(m, d), y.reshape(m, d), bonus.reshape(m, d),
                    g.reshape(m, d), gates, rwkv_ln_w[l].reshape(1, d),
                    rwkv_ln_b[l].reshape(1, d), attn_post_norm[l].reshape(1, d),
                    w_a, w_b, w_o, e_mat, tm=MERGE_ROWS)

        h = _ffn(h1.reshape(bsz, s_len, d), ffn_pre_norm[l].reshape(1, d),
                 ffn_post_norm[l].reshape(1, d), w_u, conv_w[l], conv_b[l].reshape(1, -1), w_d,
                 tm=FFN_ROWS)
    return h
```
